```python
import jax, jax.numpy as jnp
from jax import lax
import numpy as np

D_MODEL = 2048
BATCH = 2
SEQ = 8192
DEPTH = 2

GRID_W = 64
CTX_LEN = 256

HEAD_DIM = 128
N_Q_HEADS = 8
N_KV_HEADS = 2
Q_PER_KV = N_Q_HEADS // N_KV_HEADS
ATTN_W = N_Q_HEADS * HEAD_DIM
KV_W = N_KV_HEADS * HEAD_DIM
Q_BLOCK = 128
ATTN_SCALE = HEAD_DIM ** -0.5
ROPE_THETA = 10000.0
AXIS_DIM = HEAD_DIM // 2
N_FREQ = AXIS_DIM // 2

CONV_CH = 1024
CONV_WIDTH = 31

CHUNK = 128
SGU_W = D_MODEL
SGU_GROUPS = 8
SGU_GW = SGU_W // SGU_GROUPS

EV_SPLITS = [KV_W, 2 * KV_W, 2 * KV_W + ATTN_W, 2 * KV_W + 2 * ATTN_W,
             2 * KV_W + 2 * ATTN_W + 2 * CONV_CH]
EV_IN = 2 * KV_W + 2 * ATTN_W + 3 * CONV_CH
EV_MIX = ATTN_W + CONV_CH
OD_IN = 3 * SGU_W

N_EVEN = (DEPTH + 1) // 2
N_ODD = DEPTH // 2
EPS = 1e-6

kernel_name = "hybrid_gqa_conformer_sgu_prefix_block"


def rmsnorm(x, g):
    xf = x.astype(jnp.float32)
    y = xf * lax.rsqrt(jnp.mean(xf * xf, axis=-1, keepdims=True) + EPS)
    return (y * g.astype(jnp.float32)).astype(x.dtype)


def layernorm(x, g, b):
    xf = x.astype(jnp.float32)
    mu = jnp.mean(xf, axis=-1, keepdims=True)
    var = jnp.mean(jnp.square(xf - mu), axis=-1, keepdims=True)
    y = (xf - mu) * lax.rsqrt(var + EPS)
    return (y * g.astype(jnp.float32) + b.astype(jnp.float32)).astype(x.dtype)


def modulate(h, shift, scale):
    return h * (1.0 + scale) + shift


def axial_rope_tables(n):
    rows = n // GRID_W
    row = jnp.repeat(jnp.arange(rows, dtype=jnp.float32), GRID_W)
    col = jnp.tile(jnp.arange(GRID_W, dtype=jnp.float32), rows)
    inv = jnp.power(ROPE_THETA, jnp.arange(N_FREQ, dtype=jnp.float32) * (-2.0 / AXIS_DIM))
    ang = jnp.concatenate([row[:, None] * inv, col[:, None] * inv], axis=-1)
    return jnp.cos(ang), jnp.sin(ang)


def apply_rope(x, cos, sin):
    shp = x.shape
    xf = x.astype(jnp.float32).reshape(shp[:-1] + (HEAD_DIM // 2, 2))
    x1, x2 = xf[..., 0], xf[..., 1]
    cs, sn = cos[None, :, None, :], sin[None, :, None, :]
    out = jnp.stack([x1 * cs - x2 * sn, x1 * sn + x2 * cs], axis=-1)
    return out.reshape(shp).astype(x.dtype)


def latent_attention(q, k_lat, v_lat, k_ctx, v_ctx):
    bsz, n = q.shape[:2]
    k_all = jnp.concatenate([k_lat, k_ctx], axis=1)
    v_all = jnp.concatenate([v_lat, v_ctx], axis=1)
    nb = n // Q_BLOCK
    qb = q.reshape(bsz, nb, Q_BLOCK, N_KV_HEADS, Q_PER_KV, HEAD_DIM).transpose(1, 0, 2, 3, 4, 5)

    def block(qi):
        s = jnp.einsum('bqhgd,bkhd->bhgqk', qi, k_all).astype(jnp.float32) * ATTN_SCALE
        p = jax.nn.softmax(s, axis=-1)
        return jnp.einsum('bhgqk,bkhd->bqhgd', p.astype(v_all.dtype), v_all)

    out = lax.map(block, qb)
    return out.transpose(1, 0, 2, 3, 4, 5).reshape(bsz, n, ATTN_W)


def context_attention(q, k, v):
    bsz, l = q.shape[:2]
    qg = q.reshape(bsz, l, N_KV_HEADS, Q_PER_KV, HEAD_DIM)
    s = jnp.einsum('bqhgd,bkhd->bhgqk', qg, k).astype(jnp.float32) * ATTN_SCALE
    p = jax.nn.softmax(s, axis=-1)
    o = jnp.einsum('bhgqk,bkhd->bqhgd', p.astype(v.dtype), v)
    return o.reshape(bsz, l, ATTN_W)


def conformer_conv(pair, dw_w, dw_b, ln_g, ln_b):
    a, b = jnp.split(pair, 2, axis=-1)
    y = a * jax.nn.sigmoid(b)
    y = lax.conv_general_dilated(
        y, dw_w[:, None, :], window_strides=(1,),
        padding=[(CONV_WIDTH // 2, CONV_WIDTH // 2)],
        dimension_numbers=('NWC', 'WIO', 'NWC'),
        feature_group_count=CONV_CH) + dw_b
    return jax.nn.silu(layernorm(y, ln_g, ln_b))


def spatial_gating(uv, ln_g, ln_b, ws, bs):
    bsz, n = uv.shape[:2]
    u, v = jnp.split(jax.nn.gelu(uv), 2, axis=-1)
    v = layernorm(v, ln_g, ln_b).reshape(bsz, n // CHUNK, CHUNK, SGU_GROUPS, SGU_GW)
    mixed = jnp.einsum('gij,bcjgd->bcigd', ws, v) + bs.T[None, None, :, :, None]
    return u * mixed.reshape(bsz, n, SGU_W)


def setup_inputs(seed: int = 0) -> dict:
    key = jax.random.key(seed)
    ks = iter(jax.random.split(key, 32))

    def nrm(shape, scale):
        return jax.random.normal(next(ks), shape, jnp.float32) * scale

    d = D_MODEL
    return {
        "x": nrm((BATCH, SEQ, d), 1.0),
        "c": nrm((BATCH, d), 1.0),
        "ctx": nrm((BATCH, CTX_LEN, d), 1.0),
        "c_ctx": nrm((d,), 1.0),
        "ada_w": nrm((DEPTH, d, 3 * d), 0.5 * d ** -0.5),
        "ada_b": nrm((DEPTH, 3 * d), 0.02),
        "norm_g": 1.0 + nrm((DEPTH, d), 0.02),
        "ev_w_in": nrm((N_EVEN, d, EV_IN), d ** -0.5),
        "ev_q_norm": 1.0 + nrm((N_EVEN, HEAD_DIM), 0.02),
        "ev_k_norm": 1.0 + nrm((N_EVEN, HEAD_DIM), 0.02),
        "ev_dw_w": nrm((N_EVEN, CONV_WIDTH, CONV_CH), CONV_WIDTH ** -0.5),
        "ev_dw_b": nrm((N_EVEN, CONV_CH), 0.02),
        "ev_ln_g": 1.0 + nrm((N_EVEN, CONV_CH), 0.02),
        "ev_ln_b": nrm((N_EVEN, CONV_CH), 0.02),
        "ev_w_out": nrm((N_EVEN, EV_MIX, d), EV_MIX ** -0.5),
        "od_w_in": nrm((N_ODD, d, OD_IN), d ** -0.5),
        "od_ln_g": 1.0 + nrm((N_ODD, SGU_W), 0.02),
        "od_ln_b": nrm((N_ODD, SGU_W), 0.02),
        "od_ws": nrm((N_ODD, SGU_GROUPS, CHUNK, CHUNK), CHUNK ** -0.5),
        "od_bs": 1.0 + nrm((N_ODD, SGU_GROUPS, CHUNK), 0.02),
        "od_w_out": nrm((N_ODD, SGU_W, d), SGU_W ** -0.5),
        "final_g": 1.0 + nrm((d,), 0.02),
    }


def reference(x, c, ctx, c_ctx, ada_w, ada_b, norm_g, ev_w_in, ev_q_norm, ev_k_norm,
              ev_dw_w, ev_dw_b, ev_ln_g, ev_ln_b, ev_w_out, od_w_in, od_ln_g, od_ln_b,
              od_ws, od_bs, od_w_out, final_g):
    bsz, n, _ = x.shape
    cos, sin = axial_rope_tables(n)
    sc = jax.nn.silu(c)
    scc = jax.nn.silu(c_ctx)
    xc = ctx
    lc = ctx.shape[1]
    for layer in range(DEPTH):
        ctx_needed = any(j % 2 == 0 for j in range(layer + 1, DEPTH))
        is_even = layer % 2 == 0
        mod = sc @ ada_w[layer] + ada_b[layer]
        shift, scale, gate = jnp.split(mod[:, None, :], 3, axis=-1)
        h = modulate(rmsnorm(x, norm_g[layer]), shift, scale)
        if is_even or ctx_needed:
            n_mod = 3 * D_MODEL if ctx_needed else 2 * D_MODEL
            mod_c = (scc @ ada_w[layer][:, :n_mod] + ada_b[layer][:n_mod])[None, None, :]
            hc = modulate(rmsnorm(xc, norm_g[layer]),
                          mod_c[..., :D_MODEL], mod_c[..., D_MODEL:2 * D_MODEL])
        if is_even:
            e = layer // 2
            w_in = ev_w_in[e]
            k, v, q, za, glu, zb = jnp.split(h @ w_in, EV_SPLITS, axis=-1)
            q = apply_rope(rmsnorm(q.reshape(bsz, n, N_Q_HEADS, HEAD_DIM), ev_q_norm[e]), cos, sin)
            k = apply_rope(rmsnorm(k.reshape(bsz, n, N_KV_HEADS, HEAD_DIM), ev_k_norm[e]), cos, sin)
            v = v.reshape(bsz, n, N_KV_HEADS, HEAD_DIM)
            if ctx_needed:
                kc, vc, qc, zac, gluc, zbc = jnp.split(hc @ w_in, EV_SPLITS, axis=-1)
            else:
                kc, vc = jnp.split(hc @ w_in[:, :2 * KV_W], 2, axis=-1)
            kc = rmsnorm(kc.reshape(bsz, lc, N_KV_HEADS, HEAD_DIM), ev_k_norm[e])
            vc = vc.reshape(bsz, lc, N_KV_HEADS, HEAD_DIM)
            attn = latent_attention(q, k, v, kc, vc)
            conv = conformer_conv(glu, ev_dw_w[e], ev_dw_b[e], ev_ln_g[e], ev_ln_b[e])
            mix = jnp.concatenate([attn * jax.nn.silu(za), conv * jax.nn.silu(zb)], axis=-1)
            x_new = x + gate * (mix @ ev_w_out[e])
            if ctx_needed:
                qc = rmsnorm(qc.reshape(bsz, lc, N_Q_HEADS, HEAD_DIM), ev_q_norm[e])
                attn_c = context_attention(qc, kc, vc)
                conv_c = conformer_conv(gluc, ev_dw_w[e], ev_dw_b[e], ev_ln_g[e], ev_ln_b[e])
                mix_c = jnp.concatenate([attn_c * jax.nn.silu(zac), conv_c * jax.nn.silu(zbc)], axis=-1)
                xc = xc + mod_c[..., 2 * D_MODEL:] * (mix_c @ ev_w_out[e])
            x = x_new
        else:
            o = layer // 2
            p = h @ od_w_in[o]
            mixed = spatial_gating(p[..., :2 * SGU_W], od_ln_g[o], od_ln_b[o], od_ws[o], od_bs[o])
            mixed = mixed * jax.nn.silu(p[..., 2 * SGU_W:])
            x_new = x + gate * (mixed @ od_w_out[o])
            if ctx_needed:
                pc = hc @ od_w_in[o]
                mixed_c = spatial_gating(pc[..., :2 * SGU_W], od_ln_g[o], od_ln_b[o], od_ws[o], od_bs[o])
                mixed_c = mixed_c * jax.nn.silu(pc[..., 2 * SGU_W:])
                xc = xc + mod_c[..., 2 * D_MODEL:] * (mixed_c @ od_w_out[o])
            x = x_new
    return rmsnorm(x, final_g)
```

```python
import functools

import jax
import jax.numpy as jnp
from jax import lax
from jax.experimental import pallas as pl
from jax.experimental.pallas import tpu as pltpu

D_MODEL = 2048
GRID_W = 64
HEAD_DIM = 128
N_Q_HEADS = 8
N_KV_HEADS = 2
Q_PER_KV = N_Q_HEADS // N_KV_HEADS
ATTN_W = N_Q_HEADS * HEAD_DIM
KV_W = N_KV_HEADS * HEAD_DIM
ATTN_SCALE = HEAD_DIM ** -0.5
ROPE_THETA = 10000.0
AXIS_DIM = HEAD_DIM // 2
N_FREQ = AXIS_DIM // 2
CONV_CH = 1024
CONV_WIDTH = 31
CONV_HALO = 16
CHUNK = 128
SGU_W = D_MODEL
SGU_GROUPS = 8
SGU_GW = SGU_W // SGU_GROUPS
EV_IN = 2 * KV_W + 2 * ATTN_W + 3 * CONV_CH
OD_IN = 3 * SGU_W
EPS = 1e-6

LANES = 128
SUBLANES = 8
VMEM_LIMIT = 56 * 1024 * 1024

F32 = jnp.float32
BF16 = jnp.bfloat16


def _cparams(*sem):
    return pltpu.CompilerParams(dimension_semantics=sem, vmem_limit_bytes=VMEM_LIMIT)


def _resident(shape):
    nd = len(shape)
    return pl.BlockSpec(shape, lambda *_: (0,) * nd, pipeline_mode=pl.Buffered(1))


def _sigmoid(x):
    return 1.0 / (1.0 + jnp.exp(-x))


def _silu(x):
    return x * _sigmoid(x)


def _gelu_tanh(x):
    return 0.5 * x * (1.0 + jnp.tanh(0.7978845608028654 * (x + 0.044715 * (x * x * x))))


def _dot(a, b):
    return jnp.dot(a, b, preferred_element_type=F32)


def _norm_mod(x, g, shift, scale):
    ms = jnp.mean(x * x, axis=-1, keepdims=True)
    return (x * lax.rsqrt(ms + EPS) * g) * (1.0 + scale) + shift


MOD_ROWS = 3
MOD_KC = 64


def _mods_kernel(cb_ref, w_ref, b_ref, o_ref, *, tn):
    reps = tn // LANES

    def body(kc, accs):
        k0 = pl.multiple_of(kc * MOD_KC, MOD_KC)
        w = w_ref[0, pl.ds(k0, MOD_KC), :]
        out = []
        for r in range(MOD_ROWS):
            s = _silu(cb_ref[r, pl.ds(k0, MOD_KC), :])
            prod = pltpu.repeat(s, reps, axis=1) * w
            out.append(accs[r] + prod.reshape(MOD_KC // SUBLANES, SUBLANES, tn).sum(axis=0))
        return tuple(out)

    zero = jnp.zeros((SUBLANES, tn), F32)
    accs = lax.fori_loop(0, D_MODEL // MOD_KC, body, (zero,) * MOD_ROWS)
    rows = [jnp.sum(a, axis=0, keepdims=True) + b_ref[0] for a in accs]
    rows.append(jnp.zeros((SUBLANES - MOD_ROWS, tn), F32))
    o_ref[0] = jnp.concatenate(rows, axis=0)


def _mods(c, c_ctx, ada_w, ada_b, tn=1024):
    depth, d, n = ada_w.shape
    cc = jnp.concatenate([c, c_ctx[None, :]], axis=0)
    cb = jnp.broadcast_to(cc[:, :, None], (MOD_ROWS, d, LANES))
    return pl.pallas_call(
        functools.partial(_mods_kernel, tn=tn),
        out_shape=jax.ShapeDtypeStruct((depth, SUBLANES, n), F32),
        grid=(depth, n // tn),
        in_specs=[
            pl.BlockSpec((MOD_ROWS, d, LANES), lambda l, j: (0, 0, 0)),
            pl.BlockSpec((1, d, tn), lambda l, j: (l, 0, j)),
            pl.BlockSpec((1, 1, tn), lambda l, j: (l, 0, j)),
        ],
        out_specs=pl.BlockSpec((1, SUBLANES, tn), lambda l, j: (l, 0, j)),
        compiler_params=_cparams("parallel", "parallel"),
        name="mods",
    )(cb, ada_w, ada_b.reshape(depth, 1, n))


def _head_norm_rope(t, gain, cs, sn):
    ms = jnp.mean(t * t, axis=-1, keepdims=True)
    tn = t * lax.rsqrt(ms + EPS) * gain
    return tn * cs + pltpu.roll(tn, HEAD_DIM // 2, axis=1) * sn


def _inproj0_kernel(x_ref, mod_ref, g_ref, w_ref, cs_ref, sn_ref, qg_ref, kg_ref,
                    q_ref, k_ref, v_ref, za_ref, y_ref, zb_ref, h_ref):
    d = D_MODEL
    mod = mod_ref[0]
    h_ref[...] = _norm_mod(x_ref[0], g_ref[...], mod[:, 0:d], mod[:, d:2 * d]).astype(BF16)
    h = h_ref[...]
    cs = cs_ref[...]
    sn = sn_ref[...]

    kv = _dot(h, w_ref[:, 0:2 * KV_W])
    for j in range(N_KV_HEADS):
        kj = kv[:, j * HEAD_DIM:(j + 1) * HEAD_DIM]
        k_ref[0, j] = _head_norm_rope(kj, kg_ref[...], cs, sn).astype(BF16)
        v_ref[0, j] = kv[:, KV_W + j * HEAD_DIM:KV_W + (j + 1) * HEAD_DIM].astype(BF16)

    seg = 512
    heads_per_seg = seg // HEAD_DIM
    base = 2 * KV_W
    for s in range(ATTN_W // seg):
        qq = _dot(h, w_ref[:, base + s * seg:base + (s + 1) * seg])
        for j in range(heads_per_seg):
            qj = qq[:, j * HEAD_DIM:(j + 1) * HEAD_DIM]
            qj = _head_norm_rope(qj, qg_ref[...], cs, sn) * ATTN_SCALE
            q_ref[0, s * heads_per_seg + j] = qj.astype(BF16)

    base += ATTN_W
    for s in range(ATTN_W // seg):
        z = _dot(h, w_ref[:, base + s * seg:base + (s + 1) * seg])
        za_ref[0, :, s * seg:(s + 1) * seg] = _silu(z).astype(BF16)

    base += ATTN_W
    for s in range(CONV_CH // seg):
        a = _dot(h, w_ref[:, base + s * seg:base + (s + 1) * seg])
        b = _dot(h, w_ref[:, base + CONV_CH + s * seg:base + CONV_CH + (s + 1) * seg])
        y_ref[0, :, s * seg:(s + 1) * seg] = (a * _sigmoid(b)).astype(BF16)

    base += 2 * CONV_CH
    for s in range(CONV_CH // seg):
        z = _dot(h, w_ref[:, base + s * seg:base + (s + 1) * seg])
        zb_ref[0, :, s * seg:(s + 1) * seg] = _silu(z).astype(BF16)


def _inproj0(x, mods0, g, w, cs, sn, qg, kg, tm=256):
    bsz, n, d = x.shape
    row = lambda b, i: (b, i, 0)
    head = lambda b, i: (b, 0, i, 0)
    return pl.pallas_call(
        _inproj0_kernel,
        out_shape=(
            jax.ShapeDtypeStruct((bsz, N_Q_HEADS, n, HEAD_DIM), BF16),
            jax.ShapeDtypeStruct((bsz, N_KV_HEADS, n, HEAD_DIM), BF16),
            jax.ShapeDtypeStruct((bsz, N_KV_HEADS, n, HEAD_DIM), BF16),
            jax.ShapeDtypeStruct((bsz, n, ATTN_W), BF16),
            jax.ShapeDtypeStruct((bsz, n, CONV_CH), BF16),
            jax.ShapeDtypeStruct((bsz, n, CONV_CH), BF16),
        ),
        grid=(bsz, n // tm),
        in_specs=[
            pl.BlockSpec((1, tm, d), row),
            pl.BlockSpec((1, 1, 3 * d), lambda b, i: (b, 0, 0)),
            _resident((1, d)),
            _resident((d, EV_IN)),
            pl.BlockSpec((tm, HEAD_DIM), lambda b, i: (i, 0)),
            pl.BlockSpec((tm, HEAD_DIM), lambda b, i: (i, 0)),
            _resident((1, HEAD_DIM)),
            _resident((1, HEAD_DIM)),
        ],
        out_specs=(
            pl.BlockSpec((1, N_Q_HEADS, tm, HEAD_DIM), head),
            pl.BlockSpec((1, N_KV_HEADS, tm, HEAD_DIM), head),
            pl.BlockSpec((1, N_KV_HEADS, tm, HEAD_DIM), head),
            pl.BlockSpec((1, tm, ATTN_W), row),
            pl.BlockSpec((1, tm, CONV_CH), row),
            pl.BlockSpec((1, tm, CONV_CH), row),
        ),
        scratch_shapes=[pltpu.VMEM((tm, d), BF16)],
        compiler_params=_cparams("parallel", "parallel"),
        name="inproj0",
    )(x, mods0, g, w, cs, sn, qg, kg)


def _ctxkv_kernel(x_ref, mod_ref, g_ref, w_ref, kg_ref, k_ref, v_ref):
    d = D_MODEL
    mod = mod_ref[0]
    h = _norm_mod(x_ref[0], g_ref[...], mod[:, 0:d], mod[:, d:2 * d]).astype(BF16)
    kv = _dot(h, w_ref[...])
    for j in range(N_KV_HEADS):
        kj = kv[:, j * HEAD_DIM:(j + 1) * HEAD_DIM]
        ms = jnp.mean(kj * kj, axis=-1, keepdims=True)
        k_ref[0, j] = (kj * lax.rsqrt(ms + EPS) * kg_ref[...]).astype(BF16)
        v_ref[0, j] = kv[:, KV_W + j * HEAD_DIM:KV_W + (j + 1) * HEAD_DIM].astype(BF16)


def _ctxkv(ctx, mod_c, g, w_kv, kg):
    bsz, lc, d = ctx.shape
    whole = lambda b: (b, 0, 0, 0)
    return pl.pallas_call(
        _ctxkv_kernel,
        out_shape=(
            jax.ShapeDtypeStruct((bsz, N_KV_HEADS, lc, HEAD_DIM), BF16),
            jax.ShapeDtypeStruct((bsz, N_KV_HEADS, lc, HEAD_DIM), BF16),
        ),
        grid=(bsz,),
        in_specs=[
            pl.BlockSpec((1, lc, d), lambda b: (b, 0, 0)),
            pl.BlockSpec((1, 1, 3 * d), lambda b: (0, 0, 0)),
            pl.BlockSpec((1, d), lambda b: (0, 0)),
            pl.BlockSpec((d, 2 * KV_W), lambda b: (0, 0)),
            pl.BlockSpec((1, HEAD_DIM), lambda b: (0, 0)),
        ],
        out_specs=(
            pl.BlockSpec((1, N_KV_HEADS, lc, HEAD_DIM), whole),
            pl.BlockSpec((1, N_KV_HEADS, lc, HEAD_DIM), whole),
        ),
        compiler_params=_cparams("parallel"),
        name="ctxkv",
    )(ctx, mod_c, g, w_kv, kg)


def _attn_kernel(q_ref, k_ref, v_ref, kc_ref, vc_ref, za_ref, o_ref, acc_ref, m_ref, l_ref,
                 *, tq, tk):
    rows = Q_PER_KV * tq
    q = q_ref[0].reshape(rows, HEAD_DIM)
    m_ref[...] = jnp.full((rows, 1), -jnp.inf, F32)
    l_ref[...] = jnp.zeros((rows, 1), F32)
    acc_ref[...] = jnp.zeros((rows, HEAD_DIM), F32)

    def step(kb, vb):
        s = lax.dot_general(q, kb, (((1,), (1,)), ((), ())), preferred_element_type=F32)
        m_prev = m_ref[...]
        m_new = jnp.maximum(m_prev, jnp.max(s, axis=-1, keepdims=True))
        alpha = jnp.exp(m_prev - m_new)
        p = jnp.exp(s - m_new)
        l_ref[...] = alpha * l_ref[...] + jnp.sum(p, axis=-1, keepdims=True)
        acc_ref[...] = alpha * acc_ref[...] + _dot(p.astype(BF16), vb)
        m_ref[...] = m_new

    def body(c, carry):
        k0 = pl.multiple_of(c * tk, tk)
        step(k_ref[0, 0, pl.ds(k0, tk), :], v_ref[0, 0, pl.ds(k0, tk), :])
        return carry

    lax.fori_loop(0, k_ref.shape[2] // tk, body, 0)
    step(kc_ref[0, 0], vc_ref[0, 0])

    o = acc_ref[...] / l_ref[...]
    for g in range(Q_PER_KV):
        gate = za_ref[0, :, g * HEAD_DIM:(g + 1) * HEAD_DIM].astype(F32)
        o_ref[0, :, g * HEAD_DIM:(g + 1) * HEAD_DIM] = (o[g * tq:(g + 1) * tq] * gate).astype(BF16)


def _attention(q, k, v, kc, vc, za, tq=256, tk=512):
    bsz, _, n, _ = q.shape
    lc = kc.shape[2]
    gw = Q_PER_KV * HEAD_DIM
    rows = Q_PER_KV * tq
    return pl.pallas_call(
        functools.partial(_attn_kernel, tq=tq, tk=tk),
        out_shape=jax.ShapeDtypeStruct((bsz, n, ATTN_W), BF16),
        grid=(bsz, N_KV_HEADS, n // tq),
        in_specs=[
            pl.BlockSpec((1, Q_PER_KV, tq, HEAD_DIM), lambda b, h, i: (b, h, i, 0)),
            pl.BlockSpec((1, 1, n, HEAD_DIM), lambda b, h, i: (b, h, 0, 0)),
            pl.BlockSpec((1, 1, n, HEAD_DIM), lambda b, h, i: (b, h, 0, 0)),
            pl.BlockSpec((1, 1, lc, HEAD_DIM), lambda b, h, i: (b, h, 0, 0)),
            pl.BlockSpec((1, 1, lc, HEAD_DIM), lambda b, h, i: (b, h, 0, 0)),
            pl.BlockSpec((1, tq, gw), lambda b, h, i: (b, i, h)),
        ],
        out_specs=pl.BlockSpec((1, tq, gw), lambda b, h, i: (b, i, h)),
        scratch_shapes=[
            pltpu.VMEM((rows, HEAD_DIM), F32),
            pltpu.VMEM((rows, 1), F32),
            pltpu.VMEM((rows, 1), F32),
        ],
        compiler_params=_cparams("parallel", "parallel", "parallel"),
        name="attn",
    )(q, k, v, kc, vc, za)


CONV_RB = 64


def _conv_kernel(yp_ref, yc_ref, yn_ref, zb_ref, w_ref, b_ref, lg_ref, lb_ref, o_ref,
                 win_ref, cv_ref, *, tm):
    i = pl.program_id(1)
    last = pl.num_programs(1) - 1
    win_ref[0:CONV_HALO] = jnp.where(i > 0, yp_ref[0].astype(F32), 0.0)
    win_ref[CONV_HALO:CONV_HALO + tm] = yc_ref[0].astype(F32)
    win_ref[CONV_HALO + tm:2 * CONV_HALO + tm] = jnp.where(i < last, yn_ref[0].astype(F32), 0.0)

    off = CONV_HALO - CONV_WIDTH // 2
    for r in range(tm // CONV_RB):
        for c in range(CONV_CH // LANES):
            lanes = slice(c * LANES, (c + 1) * LANES)
            acc = None
            for j in range(CONV_WIDTH):
                r0 = r * CONV_RB + off + j
                term = win_ref[r0:r0 + CONV_RB, lanes] * w_ref[j:j + 1, lanes]
                acc = term if acc is None else acc + term
            cv_ref[r * CONV_RB:(r + 1) * CONV_RB, lanes] = acc + b_ref[:, lanes]

    cv = cv_ref[...]
    mu = jnp.mean(cv, axis=-1, keepdims=True)
    cen = cv - mu
    var = jnp.mean(cen * cen, axis=-1, keepdims=True)
    yln = cen * lax.rsqrt(var + EPS) * lg_ref[...] + lb_ref[...]
    o_ref[0] = (_silu(yln) * zb_ref[0].astype(F32)).astype(BF16)


def _conv(y, zb, dw_w, dw_b, ln_g, ln_b, tm=256):
    bsz, n, ch = y.shape
    hb = tm // CONV_HALO
    nh = n // CONV_HALO
    row = lambda b, i: (b, i, 0)
    return pl.pallas_call(
        functools.partial(_conv_kernel, tm=tm),
        out_shape=jax.ShapeDtypeStruct((bsz, n, ch), BF16),
        grid=(bsz, n // tm),
        in_specs=[
            pl.BlockSpec((1, CONV_HALO, ch), lambda b, i: (b, jnp.maximum(i * hb - 1, 0), 0)),
            pl.BlockSpec((1, tm, ch), row),
            pl.BlockSpec((1, CONV_HALO, ch), lambda b, i: (b, jnp.minimum((i + 1) * hb, nh - 1), 0)),
            pl.BlockSpec((1, tm, ch), row),
            pl.BlockSpec((CONV_WIDTH, ch), lambda b, i: (0, 0)),
            pl.BlockSpec((1, ch), lambda b, i: (0, 0)),
            pl.BlockSpec((1, ch), lambda b, i: (0, 0)),
            pl.BlockSpec((1, ch), lambda b, i: (0, 0)),
        ],
        out_specs=pl.BlockSpec((1, tm, ch), row),
        scratch_shapes=[
            pltpu.VMEM((tm + 2 * CONV_HALO, ch), F32),
            pltpu.VMEM((tm, ch), F32),
        ],
        compiler_params=_cparams("parallel", "parallel"),
        name="conv",
    )(y, y, y, zb, dw_w, dw_b, ln_g, ln_b)


def _outproj_kernel(*refs, k_sizes, tn, final_norm):
    parts = refs[:len(k_sizes)]
    rest = refs[len(k_sizes):]
    if final_norm:
        w_ref, x_ref, mod_ref, fg_ref, o_ref = rest
    else:
        w_ref, x_ref, mod_ref, o_ref = rest
    d = D_MODEL
    gate = mod_ref[0][:, 2 * d:3 * d]
    ss = None
    for n in range(d // tn):
        cols = slice(n * tn, (n + 1) * tn)
        y = None
        k0 = 0
        for p_ref, ks in zip(parts, k_sizes):
            t = _dot(p_ref[0], w_ref[k0:k0 + ks, cols])
            y = t if y is None else y + t
            k0 += ks
        xn = x_ref[0, :, cols] + gate[:, cols] * y
        o_ref[0, :, cols] = xn
        if final_norm:
            sq = jnp.sum(xn * xn, axis=-1, keepdims=True)
            ss = sq if ss is None else ss + sq
    if final_norm:
        o_ref[0] = o_ref[0] * lax.rsqrt(ss * (1.0 / d) + EPS) * fg_ref[...]


def _outproj(parts, w, x, mods_l, final_g=None, tm=512, tn=512):
    bsz, n, d = x.shape
    k_sizes = tuple(p.shape[-1] for p in parts)
    row = lambda b, i: (b, i, 0)
    in_specs = [pl.BlockSpec((1, tm, ks), row) for ks in k_sizes]
    in_specs += [
        _resident(w.shape),
        pl.BlockSpec((1, tm, d), row),
        pl.BlockSpec((1, 1, 3 * d), lambda b, i: (b, 0, 0)),
    ]
    args = list(parts) + [w, x, mods_l]
    if final_g is not None:
        in_specs.append(_resident((1, d)))
        args.append(final_g)
    return pl.pallas_call(
        functools.partial(_outproj_kernel, k_sizes=k_sizes, tn=tn, final_norm=final_g is not None),
        out_shape=jax.ShapeDtypeStruct((bsz, n, d), F32),
        grid=(bsz, n // tm),
        in_specs=in_specs,
        out_specs=pl.BlockSpec((1, tm, d), row),
        compiler_params=_cparams("parallel", "parallel"),
        name="outproj_final" if final_g is not None else "outproj",
    )(*args)


def _sgu_kernel(x_ref, mod_ref, g_ref, w_ref, lng_ref, lnb_ref, ws_ref, bsb_ref, o_ref,
                h_ref, v_ref, vb_ref, *, tm):
    d = D_MODEL
    mod = mod_ref[0]
    h_ref[...] = _norm_mod(x_ref[0], g_ref[...], mod[:, 0:d], mod[:, d:2 * d]).astype(BF16)
    h = h_ref[...]

    seg = 512
    for s in range(SGU_W // seg):
        cols = slice(s * seg, (s + 1) * seg)
        v_ref[:, cols] = _gelu_tanh(_dot(h, w_ref[:, SGU_W + s * seg:SGU_W + (s + 1) * seg]))
    v = v_ref[...]
    mu = jnp.mean(v, axis=-1, keepdims=True)
    cen = v - mu
    var = jnp.mean(cen * cen, axis=-1, keepdims=True)
    vb_ref[...] = (cen * lax.rsqrt(var + EPS) * lng_ref[...] + lnb_ref[...]).astype(BF16)

    for g in range(SGU_GROUPS):
        cols = slice(g * SGU_GW, (g + 1) * SGU_GW)
        u = _gelu_tanh(_dot(h, w_ref[:, g * SGU_GW:(g + 1) * SGU_GW]))
        z = _silu(_dot(h, w_ref[:, 2 * SGU_W + g * SGU_GW:2 * SGU_W + (g + 1) * SGU_GW]))
        uz = u * z
        for c in range(tm // CHUNK):
            rws = slice(c * CHUNK, (c + 1) * CHUNK)
            mixed = _dot(ws_ref[g], vb_ref[rws, cols]) + bsb_ref[:, cols]
            o_ref[0, rws, cols] = (uz[rws] * mixed).astype(BF16)


def _sgu(x, mods1, g, w, ln_g, ln_b, ws, bsb, tm=256):
    bsz, n, d = x.shape
    row = lambda b, i: (b, i, 0)
    return pl.pallas_call(
        functools.partial(_sgu_kernel, tm=tm),
        out_shape=jax.ShapeDtypeStruct((bsz, n, SGU_W), BF16),
        grid=(bsz, n // tm),
        in_specs=[
            pl.BlockSpec((1, tm, d), row),
            pl.BlockSpec((1, 1, 3 * d), lambda b, i: (b, 0, 0)),
            _resident((1, d)),
            _resident((d, OD_IN)),
            _resident((1, SGU_W)),
            _resident((1, SGU_W)),
            _resident((SGU_GROUPS, CHUNK, CHUNK)),
            _resident((CHUNK, SGU_W)),
        ],
        out_specs=pl.BlockSpec((1, tm, SGU_W), row),
        scratch_shapes=[
            pltpu.VMEM((tm, d), BF16),
            pltpu.VMEM((tm, SGU_W), F32),
            pltpu.VMEM((tm, SGU_W), BF16),
        ],
        compiler_params=_cparams("parallel", "parallel"),
        name="sgu",
    )(x, mods1, g, w, ln_g, ln_b, ws, bsb)


def _deinterleave_heads(wc, n_heads):
    lead = wc.shape[:-1]
    t = wc.reshape(lead + (n_heads, HEAD_DIM // 2, 2))
    return jnp.swapaxes(t, -1, -2).reshape(lead + (n_heads * HEAD_DIM,))


def _rope_tables(n):
    rows = n // GRID_W
    row = jnp.repeat(jnp.arange(rows, dtype=F32), GRID_W)
    col = jnp.tile(jnp.arange(GRID_W, dtype=F32), rows)
    inv = jnp.power(ROPE_THETA, jnp.arange(N_FREQ, dtype=F32) * (-2.0 / AXIS_DIM))
    ang = jnp.concatenate([row[:, None] * inv, col[:, None] * inv], axis=-1)
    cos, sin = jnp.cos(ang), jnp.sin(ang)
    return jnp.concatenate([cos, cos], axis=-1), jnp.concatenate([-sin, sin], axis=-1)


def kernel(x, c, ctx, c_ctx, ada_w, ada_b, norm_g, ev_w_in, ev_q_norm, ev_k_norm, ev_dw_w, ev_dw_b,
           ev_ln_g, ev_ln_b, ev_w_out, od_w_in, od_ln_g, od_ln_b, od_ws, od_bs, od_w_out, final_g):
    bsz, n, d = x.shape
    depth = ada_w.shape[0]

    mods = _mods(c, c_ctx, ada_w, ada_b).reshape(depth, SUBLANES, 1, 3 * d)
    mods0, mods1 = mods[0], mods[1]
    mod_c = mods0[bsz:bsz + 1]

    w_in = ev_w_in[0]
    w0 = jnp.concatenate([
        _deinterleave_heads(w_in[:, :KV_W], N_KV_HEADS),
        w_in[:, KV_W:2 * KV_W],
        _deinterleave_heads(w_in[:, 2 * KV_W:2 * KV_W + ATTN_W], N_Q_HEADS),
        w_in[:, 2 * KV_W + ATTN_W:],
    ], axis=1).astype(BF16)
    qg = _deinterleave_heads(ev_q_norm[0], 1)[None, :]
    kg = _deinterleave_heads(ev_k_norm[0], 1)[None, :]
    cs, sn = _rope_tables(n)
    g0 = norm_g[0][None, :]
    g1 = norm_g[1][None, :]

    q, k, v, za, y, zb = _inproj0(x, mods0, g0, w0, cs, sn, qg, kg)
    kc, vc = _ctxkv(ctx, mod_c, g0, w0[:, :2 * KV_W], kg)
    attn = _attention(q, k, v, kc, vc, za)
    conv = _conv(y, zb, ev_dw_w[0], ev_dw_b[0][None, :], ev_ln_g[0][None, :], ev_ln_b[0][None, :])
    x1 = _outproj([attn, conv], ev_w_out[0].astype(BF16), x, mods0)

    bsb = jnp.repeat(od_bs[0].T, SGU_GW, axis=1)
    mixed = _sgu(x1, mods1, g1, od_w_in[0].astype(BF16), od_ln_g[0][None, :], od_ln_b[0][None, :],
                 od_ws[0].astype(BF16), bsb)
    return _outproj([mixed], od_w_out[0].astype(BF16), x1, mods1, final_g=final_g[None, :])
```

```python
import functools

import jax
import jax.numpy as jnp
from jax import lax
from jax.experimental import pallas as pl
from jax.experimental.pallas import tpu as pltpu

D_MODEL = 2048
GRID_W = 64
HEAD_DIM = 128
N_Q_HEADS = 8
N_KV_HEADS = 2
Q_PER_KV = N_Q_HEADS // N_KV_HEADS
ATTN_W = N_Q_HEADS * HEAD_DIM
KV_W = N_KV_HEADS * HEAD_DIM
ATTN_SCALE = HEAD_DIM ** -0.5
Q_SCALE = ATTN_SCALE * 1.4426950408889634
V_AUG = 2 * HEAD_DIM
ROPE_THETA = 10000.0
AXIS_DIM = HEAD_DIM // 2
N_FREQ = AXIS_DIM // 2
CONV_CH = 1024
CONV_WIDTH = 31
CONV_HALO = 16
CHUNK = 128
SGU_W = D_MODEL
SGU_GROUPS = 8
SGU_GW = SGU_W // SGU_GROUPS
EV_IN = 2 * KV_W + 2 * ATTN_W + 3 * CONV_CH
OD_IN = 3 * SGU_W
EPS = 1e-6

LANES = 128
SUBLANES = 8
VMEM_LIMIT = 56 * 1024 * 1024

F32 = jnp.float32
BF16 = jnp.bfloat16


def _cparams(*sem):
    return pltpu.CompilerParams(dimension_semantics=sem, vmem_limit_bytes=VMEM_LIMIT)


def _resident(shape):
    nd = len(shape)
    return pl.BlockSpec(shape, lambda *_: (0,) * nd, pipeline_mode=pl.Buffered(1))


def _sigmoid(x):
    return 1.0 / (1.0 + jnp.exp(-x))


def _silu(x):
    return x * _sigmoid(x)


def _gelu_tanh(x):
    return 0.5 * x * (1.0 + jnp.tanh(0.7978845608028654 * (x + 0.044715 * (x * x * x))))


def _dot(a, b):
    return jnp.dot(a, b, preferred_element_type=F32)


def _norm_mod(x, g, shift, scale):
    ms = jnp.mean(x * x, axis=-1, keepdims=True)
    return (x * lax.rsqrt(ms + EPS) * g) * (1.0 + scale) + shift


MOD_ROWS = 3
MOD_KC = 64


def _mods_kernel(cb_ref, w_ref, b_ref, o_ref, *, tn):
    reps = tn // LANES

    def body(kc, accs):
        k0 = pl.multiple_of(kc * MOD_KC, MOD_KC)
        w = w_ref[0, pl.ds(k0, MOD_KC), :]
        out = []
        for r in range(MOD_ROWS):
            s = _silu(cb_ref[r, pl.ds(k0, MOD_KC), :])
            prod = pltpu.repeat(s, reps, axis=1) * w
            out.append(accs[r] + prod.reshape(MOD_KC // SUBLANES, SUBLANES, tn).sum(axis=0))
        return tuple(out)

    zero = jnp.zeros((SUBLANES, tn), F32)
    accs = lax.fori_loop(0, D_MODEL // MOD_KC, body, (zero,) * MOD_ROWS)
    rows = [jnp.sum(a, axis=0, keepdims=True) + b_ref[0] for a in accs]
    rows.append(jnp.zeros((SUBLANES - MOD_ROWS, tn), F32))
    o_ref[0] = jnp.concatenate(rows, axis=0)


def _mods(c, c_ctx, ada_w, ada_b, tn=1024):
    depth, d, n = ada_w.shape
    cc = jnp.concatenate([c, c_ctx[None, :]], axis=0)
    cb = jnp.broadcast_to(cc[:, :, None], (MOD_ROWS, d, LANES))
    return pl.pallas_call(
        functools.partial(_mods_kernel, tn=tn),
        out_shape=jax.ShapeDtypeStruct((depth, SUBLANES, n), F32),
        grid=(depth, n // tn),
        in_specs=[
            pl.BlockSpec((MOD_ROWS, d, LANES), lambda l, j: (0, 0, 0)),
            pl.BlockSpec((1, d, tn), lambda l, j: (l, 0, j)),
            pl.BlockSpec((1, 1, tn), lambda l, j: (l, 0, j)),
        ],
        out_specs=pl.BlockSpec((1, SUBLANES, tn), lambda l, j: (l, 0, j)),
        compiler_params=_cparams("parallel", "parallel"),
        name="mods",
    )(cb, ada_w, ada_b.reshape(depth, 1, n))


def _head_norm_rope(t, gain, cs, sn):
    ms = jnp.mean(t * t, axis=-1, keepdims=True)
    tn = t * lax.rsqrt(ms + EPS) * gain
    return tn * cs + pltpu.roll(tn, HEAD_DIM // 2, axis=1) * sn


def _inproj0_kernel(x_ref, mod_ref, g_ref, w_ref, cs_ref, sn_ref, qg_ref, kg_ref,
                    q_ref, k_ref, v_ref, za_ref, y_ref, zb_ref, h_ref):
    d = D_MODEL
    mod = mod_ref[0]
    h_ref[...] = _norm_mod(x_ref[0], g_ref[...], mod[:, 0:d], mod[:, d:2 * d]).astype(BF16)
    h = h_ref[...]
    cs = cs_ref[...]
    sn = sn_ref[...]

    kv = _dot(h, w_ref[:, 0:2 * KV_W])
    ones = jnp.ones((kv.shape[0], HEAD_DIM), BF16)
    for j in range(N_KV_HEADS):
        kj = kv[:, j * HEAD_DIM:(j + 1) * HEAD_DIM]
        k_ref[0, j] = _head_norm_rope(kj, kg_ref[...], cs, sn).astype(BF16)
        v_ref[0, j, :, 0:HEAD_DIM] = kv[:, KV_W + j * HEAD_DIM:KV_W + (j + 1) * HEAD_DIM].astype(BF16)
        v_ref[0, j, :, HEAD_DIM:2 * HEAD_DIM] = ones

    seg = 512
    heads_per_seg = seg // HEAD_DIM
    base = 2 * KV_W
    for s in range(ATTN_W // seg):
        qq = _dot(h, w_ref[:, base + s * seg:base + (s + 1) * seg])
        for j in range(heads_per_seg):
            qj = qq[:, j * HEAD_DIM:(j + 1) * HEAD_DIM]
            qj = _head_norm_rope(qj, qg_ref[...], cs, sn) * Q_SCALE
            q_ref[0, s * heads_per_seg + j] = qj.astype(BF16)

    base += ATTN_W
    for s in range(ATTN_W // seg):
        z = _dot(h, w_ref[:, base + s * seg:base + (s + 1) * seg])
        za_ref[0, :, s * seg:(s + 1) * seg] = _silu(z).astype(BF16)

    base += ATTN_W
    for s in range(CONV_CH // seg):
        a = _dot(h, w_ref[:, base + s * seg:base + (s + 1) * seg])
        b = _dot(h, w_ref[:, base + CONV_CH + s * seg:base + CONV_CH + (s + 1) * seg])
        y_ref[0, :, s * seg:(s + 1) * seg] = (a * _sigmoid(b)).astype(BF16)

    base += 2 * CONV_CH
    for s in range(CONV_CH // seg):
        z = _dot(h, w_ref[:, base + s * seg:base + (s + 1) * seg])
        zb_ref[0, :, s * seg:(s + 1) * seg] = _silu(z).astype(BF16)


def _inproj0(x, mods0, g, w, cs, sn, qg, kg, n_ctx, tm=256):
    bsz, n, d = x.shape
    row = lambda b, i: (b, i, 0)
    head = lambda b, i: (b, 0, i, 0)
    return pl.pallas_call(
        _inproj0_kernel,
        out_shape=(
            jax.ShapeDtypeStruct((bsz, N_Q_HEADS, n, HEAD_DIM), BF16),
            jax.ShapeDtypeStruct((bsz, N_KV_HEADS, n + n_ctx, HEAD_DIM), BF16),
            jax.ShapeDtypeStruct((bsz, N_KV_HEADS, n + n_ctx, V_AUG), BF16),
            jax.ShapeDtypeStruct((bsz, n, ATTN_W), BF16),
            jax.ShapeDtypeStruct((bsz, n, CONV_CH), BF16),
            jax.ShapeDtypeStruct((bsz, n, CONV_CH), BF16),
        ),
        grid=(bsz, n // tm),
        in_specs=[
            pl.BlockSpec((1, tm, d), row),
            pl.BlockSpec((1, 1, 3 * d), lambda b, i: (b, 0, 0)),
            _resident((1, d)),
            _resident((d, EV_IN)),
            pl.BlockSpec((tm, HEAD_DIM), lambda b, i: (i, 0)),
            pl.BlockSpec((tm, HEAD_DIM), lambda b, i: (i, 0)),
            _resident((1, HEAD_DIM)),
            _resident((1, HEAD_DIM)),
        ],
        out_specs=(
            pl.BlockSpec((1, N_Q_HEADS, tm, HEAD_DIM), head),
            pl.BlockSpec((1, N_KV_HEADS, tm, HEAD_DIM), head),
            pl.BlockSpec((1, N_KV_HEADS, tm, V_AUG), head),
            pl.BlockSpec((1, tm, ATTN_W), row),
            pl.BlockSpec((1, tm, CONV_CH), row),
            pl.BlockSpec((1, tm, CONV_CH), row),
        ),
        scratch_shapes=[pltpu.VMEM((tm, d), BF16)],
        compiler_params=_cparams("parallel", "parallel"),
        name="inproj0",
    )(x, mods0, g, w, cs, sn, qg, kg)


def _ctxkv_kernel(x_ref, mod_ref, g_ref, w_ref, kg_ref, k_in, v_in, k_ref, v_ref):
    del k_in, v_in
    d = D_MODEL
    mod = mod_ref[0]
    h = _norm_mod(x_ref[0], g_ref[...], mod[:, 0:d], mod[:, d:2 * d]).astype(BF16)
    kv = _dot(h, w_ref[...])
    ones = jnp.ones((kv.shape[0], HEAD_DIM), BF16)
    for j in range(N_KV_HEADS):
        kj = kv[:, j * HEAD_DIM:(j + 1) * HEAD_DIM]
        ms = jnp.mean(kj * kj, axis=-1, keepdims=True)
        k_ref[0, j] = (kj * lax.rsqrt(ms + EPS) * kg_ref[...]).astype(BF16)
        v_ref[0, j, :, 0:HEAD_DIM] = kv[:, KV_W + j * HEAD_DIM:KV_W + (j + 1) * HEAD_DIM].astype(BF16)
        v_ref[0, j, :, HEAD_DIM:2 * HEAD_DIM] = ones


def _ctxkv(ctx, mod_c, g, w_kv, kg, k_all, v_all):
    bsz, lc, d = ctx.shape
    n = k_all.shape[2] - lc
    tail = lambda b: (b, 0, n // lc, 0)
    return pl.pallas_call(
        _ctxkv_kernel,
        out_shape=(
            jax.ShapeDtypeStruct(k_all.shape, k_all.dtype),
            jax.ShapeDtypeStruct(v_all.shape, v_all.dtype),
        ),
        grid=(bsz,),
        in_specs=[
            pl.BlockSpec((1, lc, d), lambda b: (b, 0, 0)),
            pl.BlockSpec((1, 1, 3 * d), lambda b: (0, 0, 0)),
            pl.BlockSpec((1, d), lambda b: (0, 0)),
            pl.BlockSpec((d, 2 * KV_W), lambda b: (0, 0)),
            pl.BlockSpec((1, HEAD_DIM), lambda b: (0, 0)),
            pl.BlockSpec(memory_space=pl.ANY),
            pl.BlockSpec(memory_space=pl.ANY),
        ],
        out_specs=(
            pl.BlockSpec((1, N_KV_HEADS, lc, HEAD_DIM), tail),
            pl.BlockSpec((1, N_KV_HEADS, lc, V_AUG), tail),
        ),
        input_output_aliases={5: 0, 6: 1},
        compiler_params=_cparams("parallel"),
        name="ctxkv",
    )(ctx, mod_c, g, w_kv, kg, k_all, v_all)


def _attn_kernel(q_ref, k_ref, v_ref, za_ref, o_ref, s_ref, acc_ref, m_ref, *, tk):
    q = q_ref[0, 0]
    n_chunks = k_ref.shape[2] // tk
    m_ref[...] = jnp.full(m_ref.shape, -jnp.inf, F32)
    acc_ref[...] = jnp.zeros(acc_ref.shape, F32)

    def scores(c):
        k0 = pl.multiple_of(c * tk, tk)
        kb = k_ref[0, 0, pl.ds(k0, tk), :]
        return lax.dot_general(q, kb, (((1,), (1,)), ((), ())), preferred_element_type=F32)

    def softmax_pv(s, c):
        m_prev = m_ref[...]
        m_new = jnp.maximum(m_prev, jnp.max(s, axis=1, keepdims=True))
        alpha = jnp.exp2(m_prev - m_new)
        p = jnp.exp2(s - pltpu.repeat(m_new, tk // LANES, axis=1)).astype(BF16)
        k0 = pl.multiple_of(c * tk, tk)
        pv = _dot(p, v_ref[0, 0, pl.ds(k0, tk), :])
        acc_ref[...] = pltpu.repeat(alpha, V_AUG // LANES, axis=1) * acc_ref[...] + pv
        m_ref[...] = m_new

    s_ref[0] = scores(0)
    n_pairs = (n_chunks - 1) // 2

    def body(t, carry):
        for slot in range(2):
            c = 2 * t + slot
            s = s_ref[slot]
            s_ref[1 - slot] = scores(c + 1)
            softmax_pv(s, c)
        return carry

    lax.fori_loop(0, n_pairs, body, 0)
    c = 2 * n_pairs
    if n_chunks - c == 2:
        s = s_ref[0]
        s_ref[1] = scores(c + 1)
        softmax_pv(s, c)
        softmax_pv(s_ref[1], c + 1)
    else:
        softmax_pv(s_ref[0], c)

    acc = acc_ref[...]
    o = acc[:, 0:HEAD_DIM] / acc[:, HEAD_DIM:2 * HEAD_DIM]
    o_ref[0] = (o * za_ref[0].astype(F32)).astype(BF16)


def _attention(q, k, v, za, tq=512, tk=768):
    bsz, nh, n, _ = q.shape
    nk = k.shape[2]
    assert nk % tk == 0 and n % tq == 0
    kv_map = lambda b, h, i: (b, h // Q_PER_KV, 0, 0)
    return pl.pallas_call(
        functools.partial(_attn_kernel, tk=tk),
        out_shape=jax.ShapeDtypeStruct((bsz, n, ATTN_W), BF16),
        grid=(bsz, nh, n // tq),
        in_specs=[
            pl.BlockSpec((1, 1, tq, HEAD_DIM), lambda b, h, i: (b, h, i, 0)),
            pl.BlockSpec((1, 1, nk, HEAD_DIM), kv_map),
            pl.BlockSpec((1, 1, nk, V_AUG), kv_map),
            pl.BlockSpec((1, tq, HEAD_DIM), lambda b, h, i: (b, i, h)),
        ],
        out_specs=pl.BlockSpec((1, tq, HEAD_DIM), lambda b, h, i: (b, i, h)),
        scratch_shapes=[
            pltpu.VMEM((2, tq, tk), F32),
            pltpu.VMEM((tq, V_AUG), F32),
            pltpu.VMEM((tq, LANES), F32),
        ],
        compiler_params=_cparams("parallel", "parallel", "parallel"),
        name="attn",
    )(q, k, v, za)


CONV_RB = 64


def _conv_kernel(yp_ref, yc_ref, yn_ref, zb_ref, w_ref, b_ref, lg_ref, lb_ref, o_ref,
                 win_ref, cv_ref, *, tm):
    i = pl.program_id(1)
    last = pl.num_programs(1) - 1
    win_ref[0:CONV_HALO] = jnp.where(i > 0, yp_ref[0].astype(F32), 0.0)
    win_ref[CONV_HALO:CONV_HALO + tm] = yc_ref[0].astype(F32)
    win_ref[CONV_HALO + tm:2 * CONV_HALO + tm] = jnp.where(i < last, yn_ref[0].astype(F32), 0.0)

    off = CONV_HALO - CONV_WIDTH // 2
    for r in range(tm // CONV_RB):
        for c in range(CONV_CH // LANES):
            lanes = slice(c * LANES, (c + 1) * LANES)
            acc = None
            for j in range(CONV_WIDTH):
                r0 = r * CONV_RB + off + j
                term = win_ref[r0:r0 + CONV_RB, lanes] * w_ref[j:j + 1, lanes]
                acc = term if acc is None else acc + term
            cv_ref[r * CONV_RB:(r + 1) * CONV_RB, lanes] = acc + b_ref[:, lanes]

    cv = cv_ref[...]
    mu = jnp.mean(cv, axis=-1, keepdims=True)
    cen = cv - mu
    var = jnp.mean(cen * cen, axis=-1, keepdims=True)
    yln = cen * lax.rsqrt(var + EPS) * lg_ref[...] + lb_ref[...]
    o_ref[0] = (_silu(yln) * zb_ref[0].astype(F32)).astype(BF16)


def _conv(y, zb, dw_w, dw_b, ln_g, ln_b, tm=256):
    bsz, n, ch = y.shape
    hb = tm // CONV_HALO
    nh = n // CONV_HALO
    row = lambda b, i: (b, i, 0)
    return pl.pallas_call(
        functools.partial(_conv_kernel, tm=tm),
        out_shape=jax.ShapeDtypeStruct((bsz, n, ch), BF16),
        grid=(bsz, n // tm),
        in_specs=[
            pl.BlockSpec((1, CONV_HALO, ch), lambda b, i: (b, jnp.maximum(i * hb - 1, 0), 0)),
            pl.BlockSpec((1, tm, ch), row),
            pl.BlockSpec((1, CONV_HALO, ch), lambda b, i: (b, jnp.minimum((i + 1) * hb, nh - 1), 0)),
            pl.BlockSpec((1, tm, ch), row),
            pl.BlockSpec((CONV_WIDTH, ch), lambda b, i: (0, 0)),
            pl.BlockSpec((1, ch), lambda b, i: (0, 0)),
            pl.BlockSpec((1, ch), lambda b, i: (0, 0)),
            pl.BlockSpec((1, ch), lambda b, i: (0, 0)),
        ],
        out_specs=pl.BlockSpec((1, tm, ch), row),
        scratch_shapes=[
            pltpu.VMEM((tm + 2 * CONV_HALO, ch), F32),
            pltpu.VMEM((tm, ch), F32),
        ],
        compiler_params=_cparams("parallel", "parallel"),
        name="conv",
    )(y, y, y, zb, dw_w, dw_b, ln_g, ln_b)


def _outproj_kernel(*refs, k_sizes, tn, final_norm):
    parts = refs[:len(k_sizes)]
    rest = refs[len(k_sizes):]
    if final_norm:
        w_ref, x_ref, mod_ref, fg_ref, o_ref = rest
    else:
        w_ref, x_ref, mod_ref, o_ref = rest
    d = D_MODEL
    gate = mod_ref[0][:, 2 * d:3 * d]
    ss = None
    for n in range(d // tn):
        cols = slice(n * tn, (n + 1) * tn)
        y = None
        k0 = 0
        for p_ref, ks in zip(parts, k_sizes):
            t = _dot(p_ref[0], w_ref[k0:k0 + ks, cols])
            y = t if y is None else y + t
            k0 += ks
        xn = x_ref[0, :, cols] + gate[:, cols] * y
        o_ref[0, :, cols] = xn
        if final_norm:
            sq = jnp.sum(xn * xn, axis=-1, keepdims=True)
            ss = sq if ss is None else ss + sq
    if final_norm:
        o_ref[0] = o_ref[0] * lax.rsqrt(ss * (1.0 / d) + EPS) * fg_ref[...]


def _outproj(parts, w, x, mods_l, final_g=None, tm=512, tn=512):
    bsz, n, d = x.shape
    k_sizes = tuple(p.shape[-1] for p in parts)
    row = lambda b, i: (b, i, 0)
    in_specs = [pl.BlockSpec((1, tm, ks), row) for ks in k_sizes]
    in_specs += [
        _resident(w.shape),
        pl.BlockSpec((1, tm, d), row),
        pl.BlockSpec((1, 1, 3 * d), lambda b, i: (b, 0, 0)),
    ]
    args = list(parts) + [w, x, mods_l]
    if final_g is not None:
        in_specs.append(_resident((1, d)))
        args.append(final_g)
    return pl.pallas_call(
        functools.partial(_outproj_kernel, k_sizes=k_sizes, tn=tn, final_norm=final_g is not None),
        out_shape=jax.ShapeDtypeStruct((bsz, n, d), F32),
        grid=(bsz, n // tm),
        in_specs=in_specs,
        out_specs=pl.BlockSpec((1, tm, d), row),
        compiler_params=_cparams("parallel", "parallel"),
        name="outproj_final" if final_g is not None else "outproj",
    )(*args)


def _sgu_kernel(x_ref, mod_ref, g_ref, w_ref, lng_ref, lnb_ref, ws_ref, bsb_ref, o_ref,
                h_ref, v_ref, vb_ref, *, tm):
    d = D_MODEL
    mod = mod_ref[0]
    h_ref[...] = _norm_mod(x_ref[0], g_ref[...], mod[:, 0:d], mod[:, d:2 * d]).astype(BF16)
    h = h_ref[...]

    seg = 512
    for s in range(SGU_W // seg):
        cols = slice(s * seg, (s + 1) * seg)
        v_ref[:, cols] = _gelu_tanh(_dot(h, w_ref[:, SGU_W + s * seg:SGU_W + (s + 1) * seg]))
    v = v_ref[...]
    mu = jnp.mean(v, axis=-1, keepdims=True)
    cen = v - mu
    var = jnp.mean(cen * cen, axis=-1, keepdims=True)
    vb_ref[...] = (cen * lax.rsqrt(var + EPS) * lng_ref[...] + lnb_ref[...]).astype(BF16)

    for g in range(SGU_GROUPS):
        cols = slice(g * SGU_GW, (g + 1) * SGU_GW)
        u = _gelu_tanh(_dot(h, w_ref[:, g * SGU_GW:(g + 1) * SGU_GW]))
        z = _silu(_dot(h, w_ref[:, 2 * SGU_W + g * SGU_GW:2 * SGU_W + (g + 1) * SGU_GW]))
        uz = u * z
        for c in range(tm // CHUNK):
            rws = slice(c * CHUNK, (c + 1) * CHUNK)
            mixed = _dot(ws_ref[g], vb_ref[rws, cols]) + bsb_ref[:, cols]
            o_ref[0, rws, cols] = (uz[rws] * mixed).astype(BF16)


def _sgu(x, mods1, g, w, ln_g, ln_b, ws, bsb, tm=256):
    bsz, n, d = x.shape
    row = lambda b, i: (b, i, 0)
    return pl.pallas_call(
        functools.partial(_sgu_kernel, tm=tm),
        out_shape=jax.ShapeDtypeStruct((bsz, n, SGU_W), BF16),
        grid=(bsz, n // tm),
        in_specs=[
            pl.BlockSpec((1, tm, d), row),
            pl.BlockSpec((1, 1, 3 * d), lambda b, i: (b, 0, 0)),
            _resident((1, d)),
            _resident((d, OD_IN)),
            _resident((1, SGU_W)),
            _resident((1, SGU_W)),
            _resident((SGU_GROUPS, CHUNK, CHUNK)),
            _resident((CHUNK, SGU_W)),
        ],
        out_specs=pl.BlockSpec((1, tm, SGU_W), row),
        scratch_shapes=[
            pltpu.VMEM((tm, d), BF16),
            pltpu.VMEM((tm, SGU_W), F32),
            pltpu.VMEM((tm, SGU_W), BF16),
        ],
        compiler_params=_cparams("parallel", "parallel"),
        name="sgu",
    )(x, mods1, g, w, ln_g, ln_b, ws, bsb)


def _deinterleave_heads(wc, n_heads):
    lead = wc.shape[:-1]
    t = wc.reshape(lead + (n_heads, HEAD_DIM // 2, 2))
    return jnp.swapaxes(t, -1, -2).reshape(lead + (n_heads * HEAD_DIM,))


def _rope_tables(n):
    rows = n // GRID_W
    row = jnp.repeat(jnp.arange(rows, dtype=F32), GRID_W)
    col = jnp.tile(jnp.arange(GRID_W, dtype=F32), rows)
    inv = jnp.power(ROPE_THETA, jnp.arange(N_FREQ, dtype=F32) * (-2.0 / AXIS_DIM))
    ang = jnp.concatenate([row[:, None] * inv, col[:, None] * inv], axis=-1)
    cos, sin = jnp.cos(ang), jnp.sin(ang)
    return jnp.concatenate([cos, cos], axis=-1), jnp.concatenate([-sin, sin], axis=-1)


def kernel(x, c, ctx, c_ctx, ada_w, ada_b, norm_g, ev_w_in, ev_q_norm, ev_k_norm, ev_dw_w, ev_dw_b,
           ev_ln_g, ev_ln_b, ev_w_out, od_w_in, od_ln_g, od_ln_b, od_ws, od_bs, od_w_out, final_g):
    bsz, n, d = x.shape
    depth = ada_w.shape[0]

    mods = _mods(c, c_ctx, ada_w, ada_b).reshape(depth, SUBLANES, 1, 3 * d)
    mods0, mods1 = mods[0], mods[1]
    mod_c = mods0[bsz:bsz + 1]

    w_in = ev_w_in[0]
    w0 = jnp.concatenate([
        _deinterleave_heads(w_in[:, :KV_W], N_KV_HEADS),
        w_in[:, KV_W:2 * KV_W],
        _deinterleave_heads(w_in[:, 2 * KV_W:2 * KV_W + ATTN_W], N_Q_HEADS),
        w_in[:, 2 * KV_W + ATTN_W:],
    ], axis=1).astype(BF16)
    qg = _deinterleave_heads(ev_q_norm[0], 1)[None, :]
    kg = _deinterleave_heads(ev_k_norm[0], 1)[None, :]
    cs, sn = _rope_tables(n)
    g0 = norm_g[0][None, :]
    g1 = norm_g[1][None, :]

    q, k, v, za, y, zb = _inproj0(x, mods0, g0, w0, cs, sn, qg, kg, ctx.shape[1])
    k, v = _ctxkv(ctx, mod_c, g0, w0[:, :2 * KV_W], kg, k, v)
    attn = _attention(q, k, v, za)
    conv = _conv(y, zb, ev_dw_w[0], ev_dw_b[0][None, :], ev_ln_g[0][None, :], ev_ln_b[0][None, :])
    x1 = _outproj([attn, conv], ev_w_out[0].astype(BF16), x, mods0)

    bsb = jnp.repeat(od_bs[0].T, SGU_GW, axis=1)
    mixed = _sgu(x1, mods1, g1, od_w_in[0].astype(BF16), od_ln_g[0][None, :], od_ln_b[0][None, :],
                 od_ws[0].astype(BF16), bsb)
    return _outproj([mixed], od_w_out[0].astype(BF16), x1, mods1, final_g=final_g[None, :])
```

```python
import functools

import jax
import jax.numpy as jnp
from jax import lax
from jax.experimental import pallas as pl
from jax.experimental.pallas import tpu as pltpu

D_MODEL = 2048
GRID_W = 64
HEAD_DIM = 128
N_Q_HEADS = 8
N_KV_HEADS = 2
Q_PER_KV = N_Q_HEADS // N_KV_HEADS
ATTN_W = N_Q_HEADS * HEAD_DIM
KV_W = N_KV_HEADS * HEAD_DIM
ATTN_SCALE = HEAD_DIM ** -0.5
Q_SCALE = ATTN_SCALE * 1.4426950408889634
V_AUG = HEAD_DIM + 16
ATTN_TK = 768
ATTN_UNROLL = 4
ROPE_THETA = 10000.0
AXIS_DIM = HEAD_DIM // 2
N_FREQ = AXIS_DIM // 2
CONV_CH = 1024
CONV_WIDTH = 31
CONV_HALO = 16
CHUNK = 128
SGU_W = D_MODEL
SGU_GROUPS = 8
SGU_GW = SGU_W // SGU_GROUPS
EV_IN = 2 * KV_W + 2 * ATTN_W + 3 * CONV_CH
OD_IN = 3 * SGU_W
EPS = 1e-6

LANES = 128
SUBLANES = 8
VMEM_LIMIT = 56 * 1024 * 1024

F32 = jnp.float32
BF16 = jnp.bfloat16


def _cparams(*sem):
    return pltpu.CompilerParams(dimension_semantics=sem, vmem_limit_bytes=VMEM_LIMIT)


def _resident(shape):
    nd = len(shape)
    return pl.BlockSpec(shape, lambda *_: (0,) * nd, pipeline_mode=pl.Buffered(1))


def _sigmoid(x):
    return 1.0 / (1.0 + jnp.exp(-x))


def _silu(x):
    return x * _sigmoid(x)


def _gelu_tanh(x):
    return 0.5 * x * (1.0 + jnp.tanh(0.7978845608028654 * (x + 0.044715 * (x * x * x))))


def _dot(a, b):
    return jnp.dot(a, b, preferred_element_type=F32)


def _norm_mod(x, g, shift, scale):
    ms = jnp.mean(x * x, axis=-1, keepdims=True)
    return (x * lax.rsqrt(ms + EPS) * g) * (1.0 + scale) + shift


MOD_ROWS = 3
MOD_KC = 64


def _mods_kernel(cb_ref, w_ref, b_ref, o_ref, *, tn):
    reps = tn // LANES

    def body(kc, accs):
        k0 = pl.multiple_of(kc * MOD_KC, MOD_KC)
        w = w_ref[0, pl.ds(k0, MOD_KC), :]
        out = []
        for r in range(MOD_ROWS):
            s = _silu(cb_ref[r, pl.ds(k0, MOD_KC), :])
            prod = jnp.concatenate([s] * reps, axis=1) * w
            out.append(accs[r] + prod.reshape(MOD_KC // SUBLANES, SUBLANES, tn).sum(axis=0))
        return tuple(out)

    zero = jnp.zeros((SUBLANES, tn), F32)
    accs = lax.fori_loop(0, D_MODEL // MOD_KC, body, (zero,) * MOD_ROWS)
    rows = [jnp.sum(a, axis=0, keepdims=True) + b_ref[0] for a in accs]
    rows.append(jnp.zeros((SUBLANES - MOD_ROWS, tn), F32))
    o_ref[0] = jnp.concatenate(rows, axis=0)


def _mods(c, c_ctx, ada_w, ada_b, tn=1024):
    depth, d, n = ada_w.shape
    cc = jnp.concatenate([c, c_ctx[None, :]], axis=0)
    cb = jnp.broadcast_to(cc[:, :, None], (MOD_ROWS, d, LANES))
    return pl.pallas_call(
        functools.partial(_mods_kernel, tn=tn),
        out_shape=jax.ShapeDtypeStruct((depth, SUBLANES, n), F32),
        grid=(depth, n // tn),
        in_specs=[
            pl.BlockSpec((MOD_ROWS, d, LANES), lambda l, j: (0, 0, 0)),
            pl.BlockSpec((1, d, tn), lambda l, j: (l, 0, j)),
            pl.BlockSpec((1, 1, tn), lambda l, j: (l, 0, j)),
        ],
        out_specs=pl.BlockSpec((1, SUBLANES, tn), lambda l, j: (l, 0, j)),
        compiler_params=_cparams("parallel", "parallel"),
        name="mods",
    )(cb, ada_w, ada_b.reshape(depth, 1, n))


def _head_norm_rope(t, gain, cs, sn):
    ms = jnp.mean(t * t, axis=-1, keepdims=True)
    tn = t * lax.rsqrt(ms + EPS) * gain
    return tn * cs + pltpu.roll(tn, HEAD_DIM // 2, axis=1) * sn


def _inproj0_kernel(x_ref, mod_ref, g_ref, w_ref, cs_ref, sn_ref, qg_ref, kg_ref,
                    q_ref, k_ref, v_ref, za_ref, y_ref, zb_ref, h_ref):
    d = D_MODEL
    mod = mod_ref[0]
    h_ref[...] = _norm_mod(x_ref[0], g_ref[...], mod[:, 0:d], mod[:, d:2 * d]).astype(BF16)
    h = h_ref[...]
    cs = cs_ref[...]
    sn = sn_ref[...]

    kv = _dot(h, w_ref[:, 0:2 * KV_W])
    ones = jnp.ones((V_AUG - HEAD_DIM, kv.shape[0]), BF16)
    for j in range(N_KV_HEADS):
        kj = kv[:, j * HEAD_DIM:(j + 1) * HEAD_DIM]
        k_ref[0, j] = _head_norm_rope(kj, kg_ref[...], cs, sn).astype(BF16)
        vj = kv[:, KV_W + j * HEAD_DIM:KV_W + (j + 1) * HEAD_DIM]
        v_ref[0, j, 0, 0:HEAD_DIM, :] = vj.T.astype(BF16)
        v_ref[0, j, 0, HEAD_DIM:V_AUG, :] = ones

    seg = 512
    heads_per_seg = seg // HEAD_DIM
    base = 2 * KV_W
    for s in range(ATTN_W // seg):
        qq = _dot(h, w_ref[:, base + s * seg:base + (s + 1) * seg])
        for j in range(heads_per_seg):
            qj = qq[:, j * HEAD_DIM:(j + 1) * HEAD_DIM]
            qj = _head_norm_rope(qj, qg_ref[...], cs, sn) * Q_SCALE
            q_ref[0, s * heads_per_seg + j] = qj.astype(BF16)

    base += ATTN_W
    for s in range(ATTN_W // seg):
        z = _dot(h, w_ref[:, base + s * seg:base + (s + 1) * seg])
        za_ref[0, :, s * seg:(s + 1) * seg] = _silu(z).astype(BF16)

    base += ATTN_W
    for s in range(CONV_CH // seg):
        a = _dot(h, w_ref[:, base + s * seg:base + (s + 1) * seg])
        b = _dot(h, w_ref[:, base + CONV_CH + s * seg:base + CONV_CH + (s + 1) * seg])
        y_ref[0, :, s * seg:(s + 1) * seg] = (a * _sigmoid(b)).astype(BF16)

    base += 2 * CONV_CH
    for s in range(CONV_CH // seg):
        z = _dot(h, w_ref[:, base + s * seg:base + (s + 1) * seg])
        zb_ref[0, :, s * seg:(s + 1) * seg] = _silu(z).astype(BF16)


def _inproj0(x, mods0, g, w, cs, sn, qg, kg, n_ctx, tm=256):
    bsz, n, d = x.shape
    row = lambda b, i: (b, i, 0)
    head = lambda b, i: (b, 0, i, 0)
    tpc = ATTN_TK // tm
    return pl.pallas_call(
        _inproj0_kernel,
        out_shape=(
            jax.ShapeDtypeStruct((bsz, N_Q_HEADS, n, HEAD_DIM), BF16),
            jax.ShapeDtypeStruct((bsz, N_KV_HEADS, n + n_ctx, HEAD_DIM), BF16),
            jax.ShapeDtypeStruct((bsz, N_KV_HEADS, (n + n_ctx) // ATTN_TK, V_AUG, ATTN_TK), BF16),
            jax.ShapeDtypeStruct((bsz, n, ATTN_W), BF16),
            jax.ShapeDtypeStruct((bsz, n, CONV_CH), BF16),
            jax.ShapeDtypeStruct((bsz, n, CONV_CH), BF16),
        ),
        grid=(bsz, n // tm),
        in_specs=[
            pl.BlockSpec((1, tm, d), row),
            pl.BlockSpec((1, 1, 3 * d), lambda b, i: (b, 0, 0)),
            _resident((1, d)),
            _resident((d, EV_IN)),
            pl.BlockSpec((tm, HEAD_DIM), lambda b, i: (i, 0)),
            pl.BlockSpec((tm, HEAD_DIM), lambda b, i: (i, 0)),
            _resident((1, HEAD_DIM)),
            _resident((1, HEAD_DIM)),
        ],
        out_specs=(
            pl.BlockSpec((1, N_Q_HEADS, tm, HEAD_DIM), head),
            pl.BlockSpec((1, N_KV_HEADS, tm, HEAD_DIM), head),
            pl.BlockSpec((1, N_KV_HEADS, 1, V_AUG, tm), lambda b, i: (b, 0, i // tpc, 0, i % tpc)),
            pl.BlockSpec((1, tm, ATTN_W), row),
            pl.BlockSpec((1, tm, CONV_CH), row),
            pl.BlockSpec((1, tm, CONV_CH), row),
        ),
        scratch_shapes=[pltpu.VMEM((tm, d), BF16)],
        compiler_params=_cparams("parallel", "parallel"),
        name="inproj0",
    )(x, mods0, g, w, cs, sn, qg, kg)


def _ctxkv_kernel(x_ref, mod_ref, g_ref, w_ref, kg_ref, k_in, v_in, k_ref, v_ref):
    del k_in, v_in
    d = D_MODEL
    mod = mod_ref[0]
    h = _norm_mod(x_ref[0], g_ref[...], mod[:, 0:d], mod[:, d:2 * d]).astype(BF16)
    kv = _dot(h, w_ref[...])
    ones = jnp.ones((V_AUG - HEAD_DIM, kv.shape[0]), BF16)
    for j in range(N_KV_HEADS):
        kj = kv[:, j * HEAD_DIM:(j + 1) * HEAD_DIM]
        ms = jnp.mean(kj * kj, axis=-1, keepdims=True)
        k_ref[0, j] = (kj * lax.rsqrt(ms + EPS) * kg_ref[...]).astype(BF16)
        vj = kv[:, KV_W + j * HEAD_DIM:KV_W + (j + 1) * HEAD_DIM]
        v_ref[0, j, 0, 0:HEAD_DIM, :] = vj.T.astype(BF16)
        v_ref[0, j, 0, HEAD_DIM:V_AUG, :] = ones


def _ctxkv(ctx, mod_c, g, w_kv, kg, k_all, v_all):
    bsz, lc, d = ctx.shape
    n = k_all.shape[2] - lc
    tail = lambda b: (b, 0, n // lc, 0)
    return pl.pallas_call(
        _ctxkv_kernel,
        out_shape=(
            jax.ShapeDtypeStruct(k_all.shape, k_all.dtype),
            jax.ShapeDtypeStruct(v_all.shape, v_all.dtype),
        ),
        grid=(bsz,),
        in_specs=[
            pl.BlockSpec((1, lc, d), lambda b: (b, 0, 0)),
            pl.BlockSpec((1, 1, 3 * d), lambda b: (0, 0, 0)),
            pl.BlockSpec((1, d), lambda b: (0, 0)),
            pl.BlockSpec((d, 2 * KV_W), lambda b: (0, 0)),
            pl.BlockSpec((1, HEAD_DIM), lambda b: (0, 0)),
            pl.BlockSpec(memory_space=pl.ANY),
            pl.BlockSpec(memory_space=pl.ANY),
        ],
        out_specs=(
            pl.BlockSpec((1, N_KV_HEADS, lc, HEAD_DIM), tail),
            pl.BlockSpec((1, N_KV_HEADS, 1, V_AUG, lc), lambda b: (b, 0, n // ATTN_TK, 0, (n % ATTN_TK) // lc)),
        ),
        input_output_aliases={5: 0, 6: 1},
        compiler_params=_cparams("parallel"),
        name="ctxkv",
    )(ctx, mod_c, g, w_kv, kg, k_all, v_all)


def _attn_kernel(q_ref, qn_ref, k_ref, vt_ref, za_ref, o_ref, s_ref, acc_ref, m_ref, *, tk):
    i = pl.program_id(2)
    n_chunks = k_ref.shape[2] // tk
    assert n_chunks % 2 == 1
    q = q_ref[0, 0]

    def scores(qt, c):
        k0 = pl.multiple_of(c * tk, tk)
        kb = k_ref[0, 0, pl.ds(k0, tk), :]
        return lax.dot_general(kb, qt, (((1,), (1,)), ((), ())), preferred_element_type=F32)

    def softmax_pv(s, c):
        m_prev = m_ref[...]
        m_new = jnp.maximum(m_prev, jnp.max(s, axis=0, keepdims=True))
        alpha = jnp.exp2(m_prev - m_new)
        p = jnp.exp2(s - m_new[0:1]).astype(BF16)
        pv = _dot(vt_ref[0, 0, c], p)
        acc_ref[...] = alpha[0:1] * acc_ref[...] + pv
        m_ref[...] = m_new

    @pl.when(i == 0)
    def _():
        s_ref[0] = scores(q, 0)

    m_ref[...] = jnp.full(m_ref.shape, -jnp.inf, F32)
    acc_ref[...] = jnp.zeros(acc_ref.shape, F32)

    def sweep(cur):
        def chunk(c, slot, next_scores):
            s = s_ref[slot]
            s_ref[1 - slot] = next_scores()
            softmax_pv(s, c)

        def body(t, carry):
            for u in range(ATTN_UNROLL):
                c = ATTN_UNROLL * t + u
                chunk(c, (cur + u) % 2, functools.partial(scores, q, c + 1))
            return carry

        n_loop = (n_chunks - 1) // ATTN_UNROLL
        lax.fori_loop(0, n_loop, body, 0)
        for c in range(n_loop * ATTN_UNROLL, n_chunks - 1):
            chunk(c, (cur + c) % 2, functools.partial(scores, q, c + 1))
        chunk(n_chunks - 1, cur, functools.partial(scores, qn_ref[0, 0], 0))

    for par in range(2):
        pl.when(i % 2 == par)(functools.partial(sweep, par))

    acc = acc_ref[...]
    o = (acc[0:HEAD_DIM] / acc[HEAD_DIM:HEAD_DIM + 1]).T
    o_ref[0] = (o * za_ref[0].astype(F32)).astype(BF16)


def _attention(q, k, vt, za, tq=512):
    bsz, nh, n, _ = q.shape
    nk = k.shape[2]
    n_chunks, _, tk = vt.shape[2:]
    assert nk == n_chunks * tk and n % tq == 0
    nq = n // tq
    return pl.pallas_call(
        functools.partial(_attn_kernel, tk=tk),
        out_shape=jax.ShapeDtypeStruct((bsz, n, ATTN_W), BF16),
        grid=(bsz, nh, nq),
        in_specs=[
            pl.BlockSpec((1, 1, tq, HEAD_DIM), lambda b, h, i: (b, h, i, 0)),
            pl.BlockSpec((1, 1, tq, HEAD_DIM), lambda b, h, i: (b, h, jnp.minimum(i + 1, nq - 1), 0)),
            pl.BlockSpec((1, 1, nk, HEAD_DIM), lambda b, h, i: (b, h // Q_PER_KV, 0, 0)),
            pl.BlockSpec((1, 1, n_chunks, V_AUG, tk), lambda b, h, i: (b, h // Q_PER_KV, 0, 0, 0)),
            pl.BlockSpec((1, tq, HEAD_DIM), lambda b, h, i: (b, i, h)),
        ],
        out_specs=pl.BlockSpec((1, tq, HEAD_DIM), lambda b, h, i: (b, i, h)),
        scratch_shapes=[
            pltpu.VMEM((2, tk, tq), F32),
            pltpu.VMEM((V_AUG, tq), F32),
            pltpu.VMEM((SUBLANES, tq), F32),
        ],
        compiler_params=_cparams("parallel", "parallel", "arbitrary"),
        name="attn",
    )(q, q, k, vt, za)


CONV_RB = 128
CONV_NC = CONV_CH // LANES


def _conv_kernel(yp_ref, yc_ref, yn_ref, zb_ref, w_ref, b_ref, lg_ref, lb_ref, o_ref,
                 win_ref, cv_ref, *, tm):
    i = pl.program_id(1)
    last = pl.num_programs(1) - 1
    prev = jnp.where(i > 0, yp_ref[0].astype(F32), 0.0)
    cur = yc_ref[0].astype(F32)
    nxt = jnp.where(i < last, yn_ref[0].astype(F32), 0.0)
    for c in range(CONV_NC):
        lanes = slice(c * LANES, (c + 1) * LANES)
        win_ref[c, 0:CONV_HALO] = prev[:, lanes]
        win_ref[c, CONV_HALO:CONV_HALO + tm] = cur[:, lanes]
        win_ref[c, CONV_HALO + tm:2 * CONV_HALO + tm] = nxt[:, lanes]

    off = CONV_HALO - CONV_WIDTH // 2

    def block(idx, carry):
        c = idx % CONV_NC
        base = pl.multiple_of((idx // CONV_NC) * CONV_RB, CONV_RB)
        out = None
        for shift in range(SUBLANES):
            part = None
            for j in range(CONV_WIDTH):
                if (off + j) % SUBLANES != shift:
                    continue
                rows = pl.ds(base + (off + j - shift), CONV_RB + SUBLANES)
                term = win_ref[c, rows, :] * w_ref[c, j:j + 1, :]
                part = term if part is None else part + term
            part = part[shift:shift + CONV_RB]
            out = part if out is None else out + part
        cv_ref[c, pl.ds(base, CONV_RB), :] = out + b_ref[c]
        return carry

    lax.fori_loop(0, (tm // CONV_RB) * CONV_NC, block, 0)

    cv = jnp.concatenate([cv_ref[c] for c in range(CONV_NC)], axis=1)
    mu = jnp.mean(cv, axis=-1, keepdims=True)
    cen = cv - mu
    var = jnp.mean(cen * cen, axis=-1, keepdims=True)
    yln = cen * lax.rsqrt(var + EPS) * lg_ref[...] + lb_ref[...]
    o_ref[0] = (_silu(yln) * zb_ref[0].astype(F32)).astype(BF16)


def _conv(y, zb, dw_w, dw_b, ln_g, ln_b, tm=512):
    bsz, n, ch = y.shape
    hb = tm // CONV_HALO
    nh = n // CONV_HALO
    row = lambda b, i: (b, i, 0)
    w3 = dw_w.reshape(CONV_WIDTH, CONV_NC, LANES).transpose(1, 0, 2)
    b3 = dw_b.reshape(CONV_NC, 1, LANES)
    return pl.pallas_call(
        functools.partial(_conv_kernel, tm=tm),
        out_shape=jax.ShapeDtypeStruct((bsz, n, ch), BF16),
        grid=(bsz, n // tm),
        in_specs=[
            pl.BlockSpec((1, CONV_HALO, ch), lambda b, i: (b, jnp.maximum(i * hb - 1, 0), 0)),
            pl.BlockSpec((1, tm, ch), row),
            pl.BlockSpec((1, CONV_HALO, ch), lambda b, i: (b, jnp.minimum((i + 1) * hb, nh - 1), 0)),
            pl.BlockSpec((1, tm, ch), row),
            pl.BlockSpec((CONV_NC, CONV_WIDTH, LANES), lambda b, i: (0, 0, 0)),
            pl.BlockSpec((CONV_NC, 1, LANES), lambda b, i: (0, 0, 0)),
            pl.BlockSpec((1, ch), lambda b, i: (0, 0)),
            pl.BlockSpec((1, ch), lambda b, i: (0, 0)),
        ],
        out_specs=pl.BlockSpec((1, tm, ch), row),
        scratch_shapes=[
            pltpu.VMEM((CONV_NC, tm + 2 * CONV_HALO, LANES), F32),
            pltpu.VMEM((CONV_NC, tm, LANES), F32),
        ],
        compiler_params=_cparams("parallel", "parallel"),
        name="conv",
    )(y, y, y, zb, w3, b3, ln_g, ln_b)


def _outproj_kernel(*refs, k_sizes, tn, final_norm):
    parts = refs[:len(k_sizes)]
    rest = refs[len(k_sizes):]
    if final_norm:
        w_ref, x_ref, mod_ref, fg_ref, o_ref = rest
    else:
        w_ref, x_ref, mod_ref, o_ref = rest
    d = D_MODEL
    gate = mod_ref[0][:, 2 * d:3 * d]
    ss = None
    for n in range(d // tn):
        cols = slice(n * tn, (n + 1) * tn)
        y = None
        k0 = 0
        for p_ref, ks in zip(parts, k_sizes):
            t = _dot(p_ref[0], w_ref[k0:k0 + ks, cols])
            y = t if y is None else y + t
            k0 += ks
        xn = x_ref[0, :, cols] + gate[:, cols] * y
        o_ref[0, :, cols] = xn
        if final_norm:
            sq = jnp.sum(xn * xn, axis=-1, keepdims=True)
            ss = sq if ss is None else ss + sq
    if final_norm:
        o_ref[0] = o_ref[0] * lax.rsqrt(ss * (1.0 / d) + EPS) * fg_ref[...]


def _outproj(parts, w, x, mods_l, final_g=None, tm=512, tn=512):
    bsz, n, d = x.shape
    k_sizes = tuple(p.shape[-1] for p in parts)
    row = lambda b, i: (b, i, 0)
    in_specs = [pl.BlockSpec((1, tm, ks), row) for ks in k_sizes]
    in_specs += [
        _resident(w.shape),
        pl.BlockSpec((1, tm, d), row),
        pl.BlockSpec((1, 1, 3 * d), lambda b, i: (b, 0, 0)),
    ]
    args = list(parts) + [w, x, mods_l]
    if final_g is not None:
        in_specs.append(_resident((1, d)))
        args.append(final_g)
    return pl.pallas_call(
        functools.partial(_outproj_kernel, k_sizes=k_sizes, tn=tn, final_norm=final_g is not None),
        out_shape=jax.ShapeDtypeStruct((bsz, n, d), F32),
        grid=(bsz, n // tm),
        in_specs=in_specs,
        out_specs=pl.BlockSpec((1, tm, d), row),
        compiler_params=_cparams("parallel", "parallel"),
        name="outproj_final" if final_g is not None else "outproj",
    )(*args)


def _sgu_kernel(x_ref, mod_ref, g_ref, w_ref, lng_ref, lnb_ref, ws_ref, bsb_ref, o_ref,
                h_ref, v_ref, vb_ref, *, tm):
    d = D_MODEL
    mod = mod_ref[0]
    h_ref[...] = _norm_mod(x_ref[0], g_ref[...], mod[:, 0:d], mod[:, d:2 * d]).astype(BF16)
    h = h_ref[...]

    seg = 512
    for s in range(SGU_W // seg):
        cols = slice(s * seg, (s + 1) * seg)
        v_ref[:, cols] = _gelu_tanh(_dot(h, w_ref[:, SGU_W + s * seg:SGU_W + (s + 1) * seg]))
    v = v_ref[...]
    mu = jnp.mean(v, axis=-1, keepdims=True)
    cen = v - mu
    var = jnp.mean(cen * cen, axis=-1, keepdims=True)
    vb_ref[...] = (cen * lax.rsqrt(var + EPS) * lng_ref[...] + lnb_ref[...]).astype(BF16)

    for g in range(SGU_GROUPS):
        cols = slice(g * SGU_GW, (g + 1) * SGU_GW)
        u = _gelu_tanh(_dot(h, w_ref[:, g * SGU_GW:(g + 1) * SGU_GW]))
        z = _silu(_dot(h, w_ref[:, 2 * SGU_W + g * SGU_GW:2 * SGU_W + (g + 1) * SGU_GW]))
        uz = u * z
        for c in range(tm // CHUNK):
            rws = slice(c * CHUNK, (c + 1) * CHUNK)
            mixed = _dot(ws_ref[g], vb_ref[rws, cols]) + bsb_ref[:, cols]
            o_ref[0, rws, cols] = (uz[rws] * mixed).astype(BF16)


def _sgu(x, mods1, g, w, ln_g, ln_b, ws, bsb, tm=256):
    bsz, n, d = x.shape
    row = lambda b, i: (b, i, 0)
    return pl.pallas_call(
        functools.partial(_sgu_kernel, tm=tm),
        out_shape=jax.ShapeDtypeStruct((bsz, n, SGU_W), BF16),
        grid=(bsz, n // tm),
        in_specs=[
            pl.BlockSpec((1, tm, d), row),
            pl.BlockSpec((1, 1, 3 * d), lambda b, i: (b, 0, 0)),
            _resident((1, d)),
            _resident((d, OD_IN)),
            _resident((1, SGU_W)),
            _resident((1, SGU_W)),
            _resident((SGU_GROUPS, CHUNK, CHUNK)),
            _resident((CHUNK, SGU_W)),
        ],
        out_specs=pl.BlockSpec((1, tm, SGU_W), row),
        scratch_shapes=[
            pltpu.VMEM((tm, d), BF16),
            pltpu.VMEM((tm, SGU_W), F32),
            pltpu.VMEM((tm, SGU_W), BF16),
        ],
        compiler_params=_cparams("parallel", "parallel"),
        name="sgu",
    )(x, mods1, g, w, ln_g, ln_b, ws, bsb)


def _deinterleave_heads(wc, n_heads):
    lead = wc.shape[:-1]
    t = wc.reshape(lead + (n_heads, HEAD_DIM // 2, 2))
    return jnp.swapaxes(t, -1, -2).reshape(lead + (n_heads * HEAD_DIM,))


def _rope_tables(n):
    rows = n // GRID_W
    row = jnp.repeat(jnp.arange(rows, dtype=F32), GRID_W)
    col = jnp.tile(jnp.arange(GRID_W, dtype=F32), rows)
    inv = jnp.power(ROPE_THETA, jnp.arange(N_FREQ, dtype=F32) * (-2.0 / AXIS_DIM))
    ang = jnp.concatenate([row[:, None] * inv, col[:, None] * inv], axis=-1)
    cos, sin = jnp.cos(ang), jnp.sin(ang)
    return jnp.concatenate([cos, cos], axis=-1), jnp.concatenate([-sin, sin], axis=-1)


def kernel(x, c, ctx, c_ctx, ada_w, ada_b, norm_g, ev_w_in, ev_q_norm, ev_k_norm, ev_dw_w, ev_dw_b,
           ev_ln_g, ev_ln_b, ev_w_out, od_w_in, od_ln_g, od_ln_b, od_ws, od_bs, od_w_out, final_g):
    bsz, n, d = x.shape
    depth = ada_w.shape[0]

    mods = _mods(c, c_ctx, ada_w, ada_b).reshape(depth, SUBLANES, 1, 3 * d)
    mods0, mods1 = mods[0], mods[1]
    mod_c = mods0[bsz:bsz + 1]

    w_in = ev_w_in[0]
    w0 = jnp.concatenate([
        _deinterleave_heads(w_in[:, :KV_W], N_KV_HEADS),
        w_in[:, KV_W:2 * KV_W],
        _deinterleave_heads(w_in[:, 2 * KV_W:2 * KV_W + ATTN_W], N_Q_HEADS),
        w_in[:, 2 * KV_W + ATTN_W:],
    ], axis=1).astype(BF16)
    qg = _deinterleave_heads(ev_q_norm[0], 1)[None, :]
    kg = _deinterleave_heads(ev_k_norm[0], 1)[None, :]
    cs, sn = _rope_tables(n)
    g0 = norm_g[0][None, :]
    g1 = norm_g[1][None, :]

    q, k, v, za, y, zb = _inproj0(x, mods0, g0, w0, cs, sn, qg, kg, ctx.shape[1])
    k, v = _ctxkv(ctx, mod_c, g0, w0[:, :2 * KV_W], kg, k, v)
    attn = _attention(q, k, v, za)
    conv = _conv(y, zb, ev_dw_w[0], ev_dw_b[0], ev_ln_g[0][None, :], ev_ln_b[0][None, :])
    x1 = _outproj([attn, conv], ev_w_out[0].astype(BF16), x, mods0)

    bsb = jnp.repeat(od_bs[0].T, SGU_GW, axis=1)
    mixed = _sgu(x1, mods1, g1, od_w_in[0].astype(BF16), od_ln_g[0][None, :], od_ln_b[0][None, :],
                 od_ws[0].astype(BF16), bsb)
    return _outproj([mixed], od_w_out[0].astype(BF16), x1, mods1, final_g=final_g[None, :])
```

```python
import functools

import jax
import jax.numpy as jnp
import numpy as np
from jax import lax
from jax.experimental import pallas as pl
from jax.experimental.pallas import tpu as pltpu

D_MODEL = 2048
GRID_W = 64
HEAD_DIM = 128
N_Q_HEADS = 8
N_KV_HEADS = 2
Q_PER_KV = N_Q_HEADS // N_KV_HEADS
ATTN_W = N_Q_HEADS * HEAD_DIM
KV_W = N_KV_HEADS * HEAD_DIM
ATTN_SCALE = HEAD_DIM ** -0.5
Q_SCALE = ATTN_SCALE * 1.4426950408889634
V_AUG = HEAD_DIM + 16
ATTN_TK = 768
ATTN_UNROLL = 4
ROPE_THETA = 10000.0
AXIS_DIM = HEAD_DIM // 2
N_FREQ = AXIS_DIM // 2
CONV_CH = 1024
CONV_WIDTH = 31
CONV_HALO = 16
CHUNK = 128
SGU_W = D_MODEL
SGU_GROUPS = 8
SGU_GW = SGU_W // SGU_GROUPS
EV_IN = 2 * KV_W + 2 * ATTN_W + 3 * CONV_CH
OD_IN = 3 * SGU_W
EPS = 1e-6

LANES = 128
SUBLANES = 8
VMEM_LIMIT = 56 * 1024 * 1024

F32 = jnp.float32
BF16 = jnp.bfloat16


def _cparams(*sem):
    return pltpu.CompilerParams(dimension_semantics=sem, vmem_limit_bytes=VMEM_LIMIT)


def _resident(shape):
    nd = len(shape)
    return pl.BlockSpec(shape, lambda *_: (0,) * nd, pipeline_mode=pl.Buffered(1))


def _sigmoid(x):
    return 1.0 / (1.0 + jnp.exp(-x))


def _silu(x):
    return x * _sigmoid(x)


def _gelu_tanh(x):
    return 0.5 * x * (1.0 + jnp.tanh(0.7978845608028654 * (x + 0.044715 * (x * x * x))))


def _dot(a, b):
    return jnp.dot(a, b, preferred_element_type=F32)


def _norm_mod(x, g, shift, scale):
    ms = jnp.mean(x * x, axis=-1, keepdims=True)
    return (x * lax.rsqrt(ms + EPS) * g) * (1.0 + scale) + shift


def _mod_spec(layer):
    return pl.BlockSpec((1, SUBLANES, 3 * D_MODEL), lambda *_: (layer, 0, 0))


def _mod_row(mod_ref, row):
    return mod_ref[0, pl.ds(row, 1), :]


MOD_ROWS = 3
MOD_KC = 64


def _mods_kernel(cb_ref, w_ref, b_ref, o_ref, sc_ref, *, tn):
    reps = tn // LANES

    @pl.when(jnp.logical_and(pl.program_id(0) == 0, pl.program_id(1) == 0))
    def _():
        sc_ref[...] = _silu(cb_ref[...])

    def body(kc, accs):
        k0 = pl.multiple_of(kc * MOD_KC, MOD_KC)
        w = w_ref[0, pl.ds(k0, MOD_KC), :]
        out = []
        for r in range(MOD_ROWS):
            s = sc_ref[r, pl.ds(k0, MOD_KC), :]
            prod = jnp.concatenate([s] * reps, axis=1) * w
            out.append(accs[r] + prod.reshape(MOD_KC // SUBLANES, SUBLANES, tn).sum(axis=0))
        return tuple(out)

    zero = jnp.zeros((SUBLANES, tn), F32)
    accs = lax.fori_loop(0, D_MODEL // MOD_KC, body, (zero,) * MOD_ROWS)
    rows = [jnp.sum(a, axis=0, keepdims=True) + b_ref[0] for a in accs]
    rows.append(jnp.zeros((SUBLANES - MOD_ROWS, tn), F32))
    o_ref[0] = jnp.concatenate(rows, axis=0)


def _mods(c, c_ctx, ada_w, ada_b, tn=1024):
    depth, d, n = ada_w.shape
    cc = jnp.concatenate([c, c_ctx[None, :]], axis=0)
    cb = jnp.broadcast_to(cc[:, :, None], (MOD_ROWS, d, LANES))
    return pl.pallas_call(
        functools.partial(_mods_kernel, tn=tn),
        out_shape=jax.ShapeDtypeStruct((depth, SUBLANES, n), F32),
        grid=(depth, n // tn),
        in_specs=[
            pl.BlockSpec((MOD_ROWS, d, LANES), lambda l, j: (0, 0, 0)),
            pl.BlockSpec((1, d, tn), lambda l, j: (l, 0, j)),
            pl.BlockSpec((1, 1, tn), lambda l, j: (l, 0, j)),
        ],
        out_specs=pl.BlockSpec((1, SUBLANES, tn), lambda l, j: (l, 0, j)),
        scratch_shapes=[pltpu.VMEM((MOD_ROWS, d, LANES), F32)],
        compiler_params=_cparams("arbitrary", "arbitrary"),
        name="mods",
    )(cb, ada_w, ada_b.reshape(depth, 1, n))


def _head_norm_rope(t, gain, cs, sn):
    ms = jnp.mean(t * t, axis=-1, keepdims=True)
    tn = t * lax.rsqrt(ms + EPS) * gain
    return tn * cs + pltpu.roll(tn, HEAD_DIM // 2, axis=1) * sn


def _inproj0_kernel(x_ref, mod_ref, g_ref, w_ref, wqk_ref, cs_ref, sn_ref, qg_ref, kg_ref,
                    q_ref, k_ref, v_ref, za_ref, y_ref, zb_ref, h_ref):
    d = D_MODEL
    mod = _mod_row(mod_ref, pl.program_id(0))
    h_ref[...] = _norm_mod(x_ref[0], g_ref[...], mod[:, 0:d], mod[:, d:2 * d]).astype(BF16)
    h = h_ref[...]
    cs = cs_ref[...]
    sn = sn_ref[...]

    kk = _dot(h, wqk_ref[:, 0:KV_W])
    vv = _dot(h, w_ref[:, KV_W:2 * KV_W])
    ones = jnp.ones((V_AUG - HEAD_DIM, kk.shape[0]), BF16)
    for j in range(N_KV_HEADS):
        kj = kk[:, j * HEAD_DIM:(j + 1) * HEAD_DIM]
        k_ref[0, j] = _head_norm_rope(kj, kg_ref[...], cs, sn).astype(BF16)
        vj = vv[:, j * HEAD_DIM:(j + 1) * HEAD_DIM]
        v_ref[0, j, 0, 0:HEAD_DIM, :] = vj.T.astype(BF16)
        v_ref[0, j, 0, HEAD_DIM:V_AUG, :] = ones

    seg = 512
    heads_per_seg = seg // HEAD_DIM
    for s in range(ATTN_W // seg):
        qq = _dot(h, wqk_ref[:, KV_W + s * seg:KV_W + (s + 1) * seg])
        for j in range(heads_per_seg):
            qj = qq[:, j * HEAD_DIM:(j + 1) * HEAD_DIM]
            qj = _head_norm_rope(qj, qg_ref[...], cs, sn) * Q_SCALE
            q_ref[0, s * heads_per_seg + j] = qj.astype(BF16)

    base = 2 * KV_W + ATTN_W
    for s in range(ATTN_W // seg):
        z = _dot(h, w_ref[:, base + s * seg:base + (s + 1) * seg])
        za_ref[0, :, s * seg:(s + 1) * seg] = _silu(z).astype(BF16)

    base += ATTN_W
    for s in range(CONV_CH // seg):
        a = _dot(h, w_ref[:, base + s * seg:base + (s + 1) * seg])
        b = _dot(h, w_ref[:, base + CONV_CH + s * seg:base + CONV_CH + (s + 1) * seg])
        y_ref[0, :, s * seg:(s + 1) * seg] = (a * _sigmoid(b)).astype(BF16)

    base += 2 * CONV_CH
    for s in range(CONV_CH // seg):
        z = _dot(h, w_ref[:, base + s * seg:base + (s + 1) * seg])
        zb_ref[0, :, s * seg:(s + 1) * seg] = _silu(z).astype(BF16)


def _inproj0(x, mods, g, w, wqk, cs, sn, qg, kg, n_ctx, tm=256):
    bsz, n, d = x.shape
    row = lambda b, i: (b, i, 0)
    head = lambda b, i: (b, 0, i, 0)
    tpc = ATTN_TK // tm
    return pl.pallas_call(
        _inproj0_kernel,
        out_shape=(
            jax.ShapeDtypeStruct((bsz, N_Q_HEADS, n, HEAD_DIM), BF16),
            jax.ShapeDtypeStruct((bsz, N_KV_HEADS, n + n_ctx, HEAD_DIM), BF16),
            jax.ShapeDtypeStruct((bsz, N_KV_HEADS, (n + n_ctx) // ATTN_TK, V_AUG, ATTN_TK), BF16),
            jax.ShapeDtypeStruct((bsz, n, ATTN_W), BF16),
            jax.ShapeDtypeStruct((bsz, n, CONV_CH), BF16),
            jax.ShapeDtypeStruct((bsz, n, CONV_CH), BF16),
        ),
        grid=(bsz, n // tm),
        in_specs=[
            pl.BlockSpec((1, tm, d), row),
            _mod_spec(0),
            _resident((1, d)),
            _resident((d, EV_IN)),
            _resident((d, KV_W + ATTN_W)),
            pl.BlockSpec((tm, HEAD_DIM), lambda b, i: (i, 0)),
            pl.BlockSpec((tm, HEAD_DIM), lambda b, i: (i, 0)),
            _resident((1, HEAD_DIM)),
            _resident((1, HEAD_DIM)),
        ],
        out_specs=(
            pl.BlockSpec((1, N_Q_HEADS, tm, HEAD_DIM), head),
            pl.BlockSpec((1, N_KV_HEADS, tm, HEAD_DIM), head),
            pl.BlockSpec((1, N_KV_HEADS, 1, V_AUG, tm), lambda b, i: (b, 0, i // tpc, 0, i % tpc)),
            pl.BlockSpec((1, tm, ATTN_W), row),
            pl.BlockSpec((1, tm, CONV_CH), row),
            pl.BlockSpec((1, tm, CONV_CH), row),
        ),
        scratch_shapes=[pltpu.VMEM((tm, d), BF16)],
        compiler_params=_cparams("parallel", "parallel"),
        name="inproj0",
    )(x, mods, g, w, wqk, cs, sn, qg, kg)


def _ctxkv_kernel(x_ref, mod_ref, g_ref, wk_ref, wv_ref, kg_ref, k_in, v_in, k_ref, v_ref, *, mod_row):
    del k_in, v_in
    d = D_MODEL
    mod = _mod_row(mod_ref, mod_row)
    h = _norm_mod(x_ref[0], g_ref[...], mod[:, 0:d], mod[:, d:2 * d]).astype(BF16)
    kk = _dot(h, wk_ref[...])
    vv = _dot(h, wv_ref[...])
    ones = jnp.ones((V_AUG - HEAD_DIM, kk.shape[0]), BF16)
    for j in range(N_KV_HEADS):
        kj = kk[:, j * HEAD_DIM:(j + 1) * HEAD_DIM]
        ms = jnp.mean(kj * kj, axis=-1, keepdims=True)
        k_ref[0, j] = (kj * lax.rsqrt(ms + EPS) * kg_ref[...]).astype(BF16)
        vj = vv[:, j * HEAD_DIM:(j + 1) * HEAD_DIM]
        v_ref[0, j, 0, 0:HEAD_DIM, :] = vj.T.astype(BF16)
        v_ref[0, j, 0, HEAD_DIM:V_AUG, :] = ones


def _ctxkv(ctx, mods, g, w, wqk, kg, k_all, v_all):
    bsz, lc, d = ctx.shape
    n = k_all.shape[2] - lc
    tail = lambda b: (b, 0, n // lc, 0)
    return pl.pallas_call(
        functools.partial(_ctxkv_kernel, mod_row=bsz),
        out_shape=(
            jax.ShapeDtypeStruct(k_all.shape, k_all.dtype),
            jax.ShapeDtypeStruct(v_all.shape, v_all.dtype),
        ),
        grid=(bsz,),
        in_specs=[
            pl.BlockSpec((1, lc, d), lambda b: (b, 0, 0)),
            _mod_spec(0),
            pl.BlockSpec((1, d), lambda b: (0, 0)),
            pl.BlockSpec((d, KV_W), lambda b: (0, 0)),
            pl.BlockSpec((d, KV_W), lambda b: (0, 1)),
            pl.BlockSpec((1, HEAD_DIM), lambda b: (0, 0)),
            pl.BlockSpec(memory_space=pl.ANY),
            pl.BlockSpec(memory_space=pl.ANY),
        ],
        out_specs=(
            pl.BlockSpec((1, N_KV_HEADS, lc, HEAD_DIM), tail),
            pl.BlockSpec((1, N_KV_HEADS, 1, V_AUG, lc), lambda b: (b, 0, n // ATTN_TK, 0, (n % ATTN_TK) // lc)),
        ),
        input_output_aliases={6: 0, 7: 1},
        compiler_params=_cparams("parallel"),
        name="ctxkv",
    )(ctx, mods, g, wqk, w, kg, k_all, v_all)


def _attn_kernel(q_ref, qn_ref, k_ref, vt_ref, za_ref, o_ref, s_ref, acc_ref, m_ref, *, tk):
    i = pl.program_id(2)
    n_chunks = k_ref.shape[2] // tk
    assert n_chunks % 2 == 1
    q = q_ref[0, 0]

    def scores(qt, c):
        k0 = pl.multiple_of(c * tk, tk)
        kb = k_ref[0, 0, pl.ds(k0, tk), :]
        return lax.dot_general(kb, qt, (((1,), (1,)), ((), ())), preferred_element_type=F32)

    def softmax_pv(s, c):
        m_prev = m_ref[...]
        m_new = jnp.maximum(m_prev, jnp.max(s, axis=0, keepdims=True))
        alpha = jnp.exp2(m_prev - m_new)
        p = jnp.exp2(s - m_new[0:1]).astype(BF16)
        pv = _dot(vt_ref[0, 0, c], p)
        acc_ref[...] = alpha[0:1] * acc_ref[...] + pv
        m_ref[...] = m_new

    @pl.when(i == 0)
    def _():
        s_ref[0] = scores(q, 0)

    m_ref[...] = jnp.full(m_ref.shape, -jnp.inf, F32)
    acc_ref[...] = jnp.zeros(acc_ref.shape, F32)

    def sweep(cur):
        def chunk(c, slot, next_scores):
            s = s_ref[slot]
            s_ref[1 - slot] = next_scores()
            softmax_pv(s, c)

        def body(t, carry):
            for u in range(ATTN_UNROLL):
                c = ATTN_UNROLL * t + u
                chunk(c, (cur + u) % 2, functools.partial(scores, q, c + 1))
            return carry

        n_loop = (n_chunks - 1) // ATTN_UNROLL
        lax.fori_loop(0, n_loop, body, 0)
        for c in range(n_loop * ATTN_UNROLL, n_chunks - 1):
            chunk(c, (cur + c) % 2, functools.partial(scores, q, c + 1))
        chunk(n_chunks - 1, cur, functools.partial(scores, qn_ref[0, 0], 0))

    for par in range(2):
        pl.when(i % 2 == par)(functools.partial(sweep, par))

    acc = acc_ref[...]
    o = (acc[0:HEAD_DIM] / acc[HEAD_DIM:HEAD_DIM + 1]).T
    o_ref[0] = (o * za_ref[0].astype(F32)).astype(BF16)


def _attention(q, k, vt, za, tq=512):
    bsz, nh, n, _ = q.shape
    nk = k.shape[2]
    n_chunks, _, tk = vt.shape[2:]
    assert nk == n_chunks * tk and n % tq == 0
    nq = n // tq
    return pl.pallas_call(
        functools.partial(_attn_kernel, tk=tk),
        out_shape=jax.ShapeDtypeStruct((bsz, n, ATTN_W), BF16),
        grid=(bsz, nh, nq),
        in_specs=[
            pl.BlockSpec((1, 1, tq, HEAD_DIM), lambda b, h, i: (b, h, i, 0)),
            pl.BlockSpec((1, 1, tq, HEAD_DIM), lambda b, h, i: (b, h, jnp.minimum(i + 1, nq - 1), 0)),
            pl.BlockSpec((1, 1, nk, HEAD_DIM), lambda b, h, i: (b, h // Q_PER_KV, 0, 0)),
            pl.BlockSpec((1, 1, n_chunks, V_AUG, tk), lambda b, h, i: (b, h // Q_PER_KV, 0, 0, 0)),
            pl.BlockSpec((1, tq, HEAD_DIM), lambda b, h, i: (b, i, h)),
        ],
        out_specs=pl.BlockSpec((1, tq, HEAD_DIM), lambda b, h, i: (b, i, h)),
        scratch_shapes=[
            pltpu.VMEM((2, tk, tq), F32),
            pltpu.VMEM((V_AUG, tq), F32),
            pltpu.VMEM((SUBLANES, tq), F32),
        ],
        compiler_params=_cparams("parallel", "parallel", "arbitrary"),
        name="attn",
    )(q, q, k, vt, za)


CONV_RB = 128
CONV_NC = CONV_CH // LANES


def _conv_kernel(yp_ref, yc_ref, yn_ref, zb_ref, w_ref, b_ref, lg_ref, lb_ref, o_ref,
                 win_ref, cv_ref, *, tm):
    i = pl.program_id(1)
    last = pl.num_programs(1) - 1
    prev = jnp.where(i > 0, yp_ref[0].astype(F32), 0.0)
    cur = yc_ref[0].astype(F32)
    nxt = jnp.where(i < last, yn_ref[0].astype(F32), 0.0)
    for c in range(CONV_NC):
        lanes = slice(c * LANES, (c + 1) * LANES)
        win_ref[c, 0:CONV_HALO] = prev[:, lanes]
        win_ref[c, CONV_HALO:CONV_HALO + tm] = cur[:, lanes]
        win_ref[c, CONV_HALO + tm:2 * CONV_HALO + tm] = nxt[:, lanes]

    off = CONV_HALO - CONV_WIDTH // 2

    def block(idx, carry):
        c = idx % CONV_NC
        base = pl.multiple_of((idx // CONV_NC) * CONV_RB, CONV_RB)
        out = None
        for shift in range(SUBLANES):
            part = None
            for j in range(CONV_WIDTH):
                if (off + j) % SUBLANES != shift:
                    continue
                rows = pl.ds(base + (off + j - shift), CONV_RB + SUBLANES)
                term = win_ref[c, rows, :] * w_ref[c, j:j + 1, :]
                part = term if part is None else part + term
            part = part[shift:shift + CONV_RB]
            out = part if out is None else out + part
        cv_ref[c, pl.ds(base, CONV_RB), :] = out + b_ref[c]
        return carry

    lax.fori_loop(0, (tm // CONV_RB) * CONV_NC, block, 0)

    cv = jnp.concatenate([cv_ref[c] for c in range(CONV_NC)], axis=1)
    mu = jnp.mean(cv, axis=-1, keepdims=True)
    cen = cv - mu
    var = jnp.mean(cen * cen, axis=-1, keepdims=True)
    yln = cen * lax.rsqrt(var + EPS) * lg_ref[...] + lb_ref[...]
    o_ref[0] = (_silu(yln) * zb_ref[0].astype(F32)).astype(BF16)


def _conv(y, zb, dw_w, dw_b, ln_g, ln_b, tm=512):
    bsz, n, ch = y.shape
    hb = tm // CONV_HALO
    nh = n // CONV_HALO
    row = lambda b, i: (b, i, 0)
    w3 = dw_w.reshape(CONV_WIDTH, CONV_NC, LANES).transpose(1, 0, 2)
    b3 = dw_b.reshape(CONV_NC, 1, LANES)
    return pl.pallas_call(
        functools.partial(_conv_kernel, tm=tm),
        out_shape=jax.ShapeDtypeStruct((bsz, n, ch), BF16),
        grid=(bsz, n // tm),
        in_specs=[
            pl.BlockSpec((1, CONV_HALO, ch), lambda b, i: (b, jnp.maximum(i * hb - 1, 0), 0)),
            pl.BlockSpec((1, tm, ch), row),
            pl.BlockSpec((1, CONV_HALO, ch), lambda b, i: (b, jnp.minimum((i + 1) * hb, nh - 1), 0)),
            pl.BlockSpec((1, tm, ch), row),
            pl.BlockSpec((CONV_NC, CONV_WIDTH, LANES), lambda b, i: (0, 0, 0)),
            pl.BlockSpec((CONV_NC, 1, LANES), lambda b, i: (0, 0, 0)),
            pl.BlockSpec((1, ch), lambda b, i: (0, 0)),
            pl.BlockSpec((1, ch), lambda b, i: (0, 0)),
        ],
        out_specs=pl.BlockSpec((1, tm, ch), row),
        scratch_shapes=[
            pltpu.VMEM((CONV_NC, tm + 2 * CONV_HALO, LANES), F32),
            pltpu.VMEM((CONV_NC, tm, LANES), F32),
        ],
        compiler_params=_cparams("parallel", "parallel"),
        name="conv",
    )(y, y, y, zb, w3, b3, ln_g, ln_b)


def _outproj_kernel(*refs, k_sizes, tn, final_norm):
    parts = refs[:len(k_sizes)]
    rest = refs[len(k_sizes):]
    if final_norm:
        w_ref, x_ref, mod_ref, fg_ref, o_ref = rest
    else:
        w_ref, x_ref, mod_ref, o_ref = rest
    d = D_MODEL
    gate = _mod_row(mod_ref, pl.program_id(0))[:, 2 * d:3 * d]
    ss = None
    for n in range(d // tn):
        cols = slice(n * tn, (n + 1) * tn)
        y = None
        k0 = 0
        for p_ref, ks in zip(parts, k_sizes):
            t = _dot(p_ref[0], w_ref[k0:k0 + ks, cols])
            y = t if y is None else y + t
            k0 += ks
        xn = x_ref[0, :, cols] + gate[:, cols] * y
        o_ref[0, :, cols] = xn
        if final_norm:
            sq = jnp.sum(xn * xn, axis=-1, keepdims=True)
            ss = sq if ss is None else ss + sq
    if final_norm:
        o_ref[0] = o_ref[0] * lax.rsqrt(ss * (1.0 / d) + EPS) * fg_ref[...]


def _outproj(parts, w, x, mods, layer, final_g=None, tm=512, tn=512):
    bsz, n, d = x.shape
    k_sizes = tuple(p.shape[-1] for p in parts)
    row = lambda b, i: (b, i, 0)
    in_specs = [pl.BlockSpec((1, tm, ks), row) for ks in k_sizes]
    in_specs += [
        _resident(w.shape),
        pl.BlockSpec((1, tm, d), row),
        _mod_spec(layer),
    ]
    args = list(parts) + [w, x, mods]
    if final_g is not None:
        in_specs.append(_resident((1, d)))
        args.append(final_g)
    return pl.pallas_call(
        functools.partial(_outproj_kernel, k_sizes=k_sizes, tn=tn, final_norm=final_g is not None),
        out_shape=jax.ShapeDtypeStruct((bsz, n, d), F32),
        grid=(bsz, n // tm),
        in_specs=in_specs,
        out_specs=pl.BlockSpec((1, tm, d), row),
        compiler_params=_cparams("parallel", "parallel"),
        name="outproj_final" if final_g is not None else "outproj",
    )(*args)


def _sgu_kernel(x_ref, mod_ref, g_ref, w_ref, lng_ref, lnb_ref, ws_ref, bsb_ref, o_ref,
                h_ref, v_ref, vb_ref, *, tm):
    d = D_MODEL
    mod = _mod_row(mod_ref, pl.program_id(0))
    h_ref[...] = _norm_mod(x_ref[0], g_ref[...], mod[:, 0:d], mod[:, d:2 * d]).astype(BF16)
    h = h_ref[...]

    seg = 512
    for s in range(SGU_W // seg):
        cols = slice(s * seg, (s + 1) * seg)
        v_ref[:, cols] = _gelu_tanh(_dot(h, w_ref[:, SGU_W + s * seg:SGU_W + (s + 1) * seg]))
    v = v_ref[...]
    mu = jnp.mean(v, axis=-1, keepdims=True)
    cen = v - mu
    var = jnp.mean(cen * cen, axis=-1, keepdims=True)
    vb_ref[...] = (cen * lax.rsqrt(var + EPS) * lng_ref[...] + lnb_ref[...]).astype(BF16)

    for g in range(SGU_GROUPS):
        cols = slice(g * SGU_GW, (g + 1) * SGU_GW)
        u = _gelu_tanh(_dot(h, w_ref[:, g * SGU_GW:(g + 1) * SGU_GW]))
        z = _silu(_dot(h, w_ref[:, 2 * SGU_W + g * SGU_GW:2 * SGU_W + (g + 1) * SGU_GW]))
        uz = u * z
        for c in range(tm // CHUNK):
            rws = slice(c * CHUNK, (c + 1) * CHUNK)
            mixed = _dot(ws_ref[g], vb_ref[rws, cols]) + bsb_ref[:, cols]
            o_ref[0, rws, cols] = (uz[rws] * mixed).astype(BF16)


def _sgu(x, mods, g, w, ln_g, ln_b, ws, bsb, tm=256):
    bsz, n, d = x.shape
    row = lambda b, i: (b, i, 0)
    return pl.pallas_call(
        functools.partial(_sgu_kernel, tm=tm),
        out_shape=jax.ShapeDtypeStruct((bsz, n, SGU_W), BF16),
        grid=(bsz, n // tm),
        in_specs=[
            pl.BlockSpec((1, tm, d), row),
            _mod_spec(1),
            _resident((1, d)),
            _resident((d, OD_IN)),
            _resident((1, SGU_W)),
            _resident((1, SGU_W)),
            _resident((SGU_GROUPS, CHUNK, CHUNK)),
            _resident((CHUNK, SGU_W)),
        ],
        out_specs=pl.BlockSpec((1, tm, SGU_W), row),
        scratch_shapes=[
            pltpu.VMEM((tm, d), BF16),
            pltpu.VMEM((tm, SGU_W), F32),
            pltpu.VMEM((tm, SGU_W), BF16),
        ],
        compiler_params=_cparams("parallel", "parallel"),
        name="sgu",
    )(x, mods, g, w, ln_g, ln_b, ws, bsb)


def _deinterleave_heads(wc, n_heads):
    lead = wc.shape[:-1]
    t = wc.reshape(lead + (n_heads, HEAD_DIM // 2, 2))
    return jnp.swapaxes(t, -1, -2).reshape(lead + (n_heads * HEAD_DIM,))


def _rope_tables(n):
    rows = n // GRID_W
    row = np.repeat(np.arange(rows, dtype=np.float64), GRID_W)
    col = np.tile(np.arange(GRID_W, dtype=np.float64), rows)
    inv = np.power(ROPE_THETA, np.arange(N_FREQ, dtype=np.float64) * (-2.0 / AXIS_DIM))
    ang = np.concatenate([row[:, None] * inv, col[:, None] * inv], axis=-1)
    cos, sin = np.cos(ang), np.sin(ang)
    cs = np.concatenate([cos, cos], axis=-1).astype(np.float32)
    sn = np.concatenate([-sin, sin], axis=-1).astype(np.float32)
    return jnp.asarray(cs), jnp.asarray(sn)


def kernel(x, c, ctx, c_ctx, ada_w, ada_b, norm_g, ev_w_in, ev_q_norm, ev_k_norm, ev_dw_w, ev_dw_b,
           ev_ln_g, ev_ln_b, ev_w_out, od_w_in, od_ln_g, od_ln_b, od_ws, od_bs, od_w_out, final_g):
    bsz, n, d = x.shape
    assert bsz + 1 == MOD_ROWS

    mods = _mods(c, c_ctx, ada_w, ada_b)

    w_in = ev_w_in[0]
    w0 = w_in.astype(BF16)
    wqk = jnp.concatenate([
        _deinterleave_heads(w_in[:, :KV_W], N_KV_HEADS),
        _deinterleave_heads(w_in[:, 2 * KV_W:2 * KV_W + ATTN_W], N_Q_HEADS),
    ], axis=1).astype(BF16)
    qg = _deinterleave_heads(ev_q_norm[0], 1)[None, :]
    kg = _deinterleave_heads(ev_k_norm[0], 1)[None, :]
    cs, sn = _rope_tables(n)
    g0 = norm_g[0][None, :]
    g1 = norm_g[1][None, :]

    q, k, v, za, y, zb = _inproj0(x, mods, g0, w0, wqk, cs, sn, qg, kg, ctx.shape[1])
    k, v = _ctxkv(ctx, mods, g0, w0, wqk, kg, k, v)
    attn = _attention(q, k, v, za)
    conv = _conv(y, zb, ev_dw_w[0], ev_dw_b[0], ev_ln_g[0][None, :], ev_ln_b[0][None, :])
    x1 = _outproj([attn, conv], ev_w_out[0].astype(BF16), x, mods, 0)

    bsb = jnp.repeat(od_bs[0].T, SGU_GW, axis=1)
    mixed = _sgu(x1, mods, g1, od_w_in[0].astype(BF16), od_ln_g[0][None, :], od_ln_b[0][None, :],
                 od_ws[0].astype(BF16), bsb)
    return _outproj([mixed], od_w_out[0].astype(BF16), x1, mods, 1, final_g=final_g[None, :])
```

```python
import functools

import jax
import jax.numpy as jnp
import numpy as np
from jax import lax
from jax.experimental import pallas as pl
from jax.experimental.pallas import tpu as pltpu

D_MODEL = 2048
GRID_W = 64
HEAD_DIM = 128
N_Q_HEADS = 8
N_KV_HEADS = 2
Q_PER_KV = N_Q_HEADS // N_KV_HEADS
ATTN_W = N_Q_HEADS * HEAD_DIM
KV_W = N_KV_HEADS * HEAD_DIM
ATTN_SCALE = HEAD_DIM ** -0.5
Q_SCALE = ATTN_SCALE * 1.4426950408889634
V_AUG = HEAD_DIM + 16
ATTN_TK = 768
ATTN_UNROLL = 4
ROPE_THETA = 10000.0
AXIS_DIM = HEAD_DIM // 2
N_FREQ = AXIS_DIM // 2
CONV_CH = 1024
CONV_WIDTH = 31
CONV_HALO = 16
CHUNK = 128
SGU_W = D_MODEL
SGU_GROUPS = 8
SGU_GW = SGU_W // SGU_GROUPS
EV_IN = 2 * KV_W + 2 * ATTN_W + 3 * CONV_CH
OD_IN = 3 * SGU_W
EPS = 1e-6

LANES = 128
SUBLANES = 8
VMEM_LIMIT = 56 * 1024 * 1024

F32 = jnp.float32
BF16 = jnp.bfloat16


def _cparams(*sem):
    return pltpu.CompilerParams(dimension_semantics=sem, vmem_limit_bytes=VMEM_LIMIT)


def _resident(shape):
    nd = len(shape)
    return pl.BlockSpec(shape, lambda *_: (0,) * nd, pipeline_mode=pl.Buffered(1))


def _sigmoid(x):
    return 1.0 / (1.0 + jnp.exp(-x))


def _silu(x):
    return x * _sigmoid(x)


def _gelu_tanh(x):
    return 0.5 * x * (1.0 + jnp.tanh(0.7978845608028654 * (x + 0.044715 * (x * x * x))))


def _dot(a, b):
    return jnp.dot(a, b, preferred_element_type=F32)


def _norm_mod(x, g, shift, scale):
    ms = jnp.mean(x * x, axis=-1, keepdims=True)
    return (x * lax.rsqrt(ms + EPS) * g) * (1.0 + scale) + shift


def _mod_spec(layer):
    return pl.BlockSpec((1, SUBLANES, 3 * D_MODEL), lambda *_: (layer, 0, 0))


def _mod_row(mod_ref, row):
    return mod_ref[0, pl.ds(row, 1), :]


MOD_ROWS = 3
MOD_KC = 64


def _mods_kernel(cb_ref, w_ref, b_ref, o_ref, sc_ref, *, tn):
    reps = tn // LANES

    @pl.when(jnp.logical_and(pl.program_id(0) == 0, pl.program_id(1) == 0))
    def _():
        sc_ref[...] = _silu(cb_ref[...])

    def body(kc, accs):
        k0 = pl.multiple_of(kc * MOD_KC, MOD_KC)
        w = w_ref[0, pl.ds(k0, MOD_KC), :]
        out = []
        for r in range(MOD_ROWS):
            s = sc_ref[r, pl.ds(k0, MOD_KC), :]
            prod = jnp.concatenate([s] * reps, axis=1) * w
            out.append(accs[r] + prod.reshape(MOD_KC // SUBLANES, SUBLANES, tn).sum(axis=0))
        return tuple(out)

    zero = jnp.zeros((SUBLANES, tn), F32)
    accs = lax.fori_loop(0, D_MODEL // MOD_KC, body, (zero,) * MOD_ROWS)
    rows = [jnp.sum(a, axis=0, keepdims=True) + b_ref[0] for a in accs]
    rows.append(jnp.zeros((SUBLANES - MOD_ROWS, tn), F32))
    o_ref[0] = jnp.concatenate(rows, axis=0)


def _mods(c, c_ctx, ada_w, ada_b, tn=1024):
    depth, d, n = ada_w.shape
    cc = jnp.concatenate([c, c_ctx[None, :]], axis=0)
    cb = jnp.broadcast_to(cc[:, :, None], (MOD_ROWS, d, LANES))
    return pl.pallas_call(
        functools.partial(_mods_kernel, tn=tn),
        out_shape=jax.ShapeDtypeStruct((depth, SUBLANES, n), F32),
        grid=(depth, n // tn),
        in_specs=[
            pl.BlockSpec((MOD_ROWS, d, LANES), lambda l, j: (0, 0, 0)),
            pl.BlockSpec((1, d, tn), lambda l, j: (l, 0, j)),
            pl.BlockSpec((1, 1, tn), lambda l, j: (l, 0, j)),
        ],
        out_specs=pl.BlockSpec((1, SUBLANES, tn), lambda l, j: (l, 0, j)),
        scratch_shapes=[pltpu.VMEM((MOD_ROWS, d, LANES), F32)],
        compiler_params=_cparams("arbitrary", "arbitrary"),
        name="mods",
    )(cb, ada_w, ada_b.reshape(depth, 1, n))


def _head_norm_rope(t, gain, cs, sn):
    ms = jnp.mean(t * t, axis=-1, keepdims=True)
    tn = t * lax.rsqrt(ms + EPS) * gain
    return tn * cs + pltpu.roll(tn, HEAD_DIM // 2, axis=1) * sn


def _inproj0_kernel(x_ref, mod_ref, g_ref, w_ref, wqk_ref, cs_ref, sn_ref, qg_ref, kg_ref,
                    q_ref, k_ref, v_ref, za_ref, y_ref, zb_ref, h_ref):
    d = D_MODEL
    mod = _mod_row(mod_ref, pl.program_id(0))
    h_ref[...] = _norm_mod(x_ref[0], g_ref[...], mod[:, 0:d], mod[:, d:2 * d]).astype(BF16)
    h = h_ref[...]
    cs = cs_ref[...]
    sn = sn_ref[...]

    kk = _dot(h, wqk_ref[:, 0:KV_W])
    vv = _dot(h, w_ref[:, KV_W:2 * KV_W])
    ones = jnp.ones((V_AUG - HEAD_DIM, kk.shape[0]), BF16)
    for j in range(N_KV_HEADS):
        kj = kk[:, j * HEAD_DIM:(j + 1) * HEAD_DIM]
        k_ref[0, j] = _head_norm_rope(kj, kg_ref[...], cs, sn).astype(BF16)
        vj = vv[:, j * HEAD_DIM:(j + 1) * HEAD_DIM]
        v_ref[0, j, 0, 0:HEAD_DIM, :] = vj.T.astype(BF16)
        v_ref[0, j, 0, HEAD_DIM:V_AUG, :] = ones

    seg = 512
    heads_per_seg = seg // HEAD_DIM
    for s in range(ATTN_W // seg):
        qq = _dot(h, wqk_ref[:, KV_W + s * seg:KV_W + (s + 1) * seg])
        for j in range(heads_per_seg):
            qj = qq[:, j * HEAD_DIM:(j + 1) * HEAD_DIM]
            qj = _head_norm_rope(qj, qg_ref[...], cs, sn) * Q_SCALE
            q_ref[0, s * heads_per_seg + j] = qj.astype(BF16)

    base = 2 * KV_W + ATTN_W
    for s in range(ATTN_W // seg):
        z = _dot(h, w_ref[:, base + s * seg:base + (s + 1) * seg])
        za_ref[0, :, s * seg:(s + 1) * seg] = _silu(z).astype(BF16)

    base += ATTN_W
    for s in range(CONV_CH // seg):
        a = _dot(h, w_ref[:, base + s * seg:base + (s + 1) * seg])
        b = _dot(h, w_ref[:, base + CONV_CH + s * seg:base + CONV_CH + (s + 1) * seg])
        y_ref[0, :, s * seg:(s + 1) * seg] = (a * _sigmoid(b)).astype(BF16)

    base += 2 * CONV_CH
    for s in range(CONV_CH // seg):
        z = _dot(h, w_ref[:, base + s * seg:base + (s + 1) * seg])
        zb_ref[0, :, s * seg:(s + 1) * seg] = _silu(z).astype(BF16)


def _inproj0(x, mods, g, w, wqk, cs, sn, qg, kg, n_ctx, tm=256):
    bsz, n, d = x.shape
    row = lambda b, i: (b, i, 0)
    head = lambda b, i: (b, 0, i, 0)
    tpc = ATTN_TK // tm
    return pl.pallas_call(
        _inproj0_kernel,
        out_shape=(
            jax.ShapeDtypeStruct((bsz, N_Q_HEADS, n, HEAD_DIM), BF16),
            jax.ShapeDtypeStruct((bsz, N_KV_HEADS, n + n_ctx, HEAD_DIM), BF16),
            jax.ShapeDtypeStruct((bsz, N_KV_HEADS, (n + n_ctx) // ATTN_TK, V_AUG, ATTN_TK), BF16),
            jax.ShapeDtypeStruct((bsz, n, ATTN_W), BF16),
            jax.ShapeDtypeStruct((bsz, n, CONV_CH), BF16),
            jax.ShapeDtypeStruct((bsz, n, CONV_CH), BF16),
        ),
        grid=(bsz, n // tm),
        in_specs=[
            pl.BlockSpec((1, tm, d), row),
            _mod_spec(0),
            _resident((1, d)),
            _resident((d, EV_IN)),
            _resident((d, KV_W + ATTN_W)),
            pl.BlockSpec((tm, HEAD_DIM), lambda b, i: (i, 0)),
            pl.BlockSpec((tm, HEAD_DIM), lambda b, i: (i, 0)),
            _resident((1, HEAD_DIM)),
            _resident((1, HEAD_DIM)),
        ],
        out_specs=(
            pl.BlockSpec((1, N_Q_HEADS, tm, HEAD_DIM), head),
            pl.BlockSpec((1, N_KV_HEADS, tm, HEAD_DIM), head),
            pl.BlockSpec((1, N_KV_HEADS, 1, V_AUG, tm), lambda b, i: (b, 0, i // tpc, 0, i % tpc)),
            pl.BlockSpec((1, tm, ATTN_W), row),
            pl.BlockSpec((1, tm, CONV_CH), row),
            pl.BlockSpec((1, tm, CONV_CH), row),
        ),
        scratch_shapes=[pltpu.VMEM((tm, d), BF16)],
        compiler_params=_cparams("parallel", "parallel"),
        name="inproj0",
    )(x, mods, g, w, wqk, cs, sn, qg, kg)


def _ctxkv_kernel(x_ref, mod_ref, g_ref, wk_ref, wv_ref, kg_ref, k_in, v_in, k_ref, v_ref, *, mod_row):
    del k_in, v_in
    d = D_MODEL
    mod = _mod_row(mod_ref, mod_row)
    h = _norm_mod(x_ref[0], g_ref[...], mod[:, 0:d], mod[:, d:2 * d]).astype(BF16)
    kk = _dot(h, wk_ref[...])
    vv = _dot(h, wv_ref[...])
    ones = jnp.ones((V_AUG - HEAD_DIM, kk.shape[0]), BF16)
    for j in range(N_KV_HEADS):
        kj = kk[:, j * HEAD_DIM:(j + 1) * HEAD_DIM]
        ms = jnp.mean(kj * kj, axis=-1, keepdims=True)
        k_ref[0, j] = (kj * lax.rsqrt(ms + EPS) * kg_ref[...]).astype(BF16)
        vj = vv[:, j * HEAD_DIM:(j + 1) * HEAD_DIM]
        v_ref[0, j, 0, 0:HEAD_DIM, :] = vj.T.astype(BF16)
        v_ref[0, j, 0, HEAD_DIM:V_AUG, :] = ones


def _ctxkv(ctx, mods, g, w, wqk, kg, k_all, v_all):
    bsz, lc, d = ctx.shape
    n = k_all.shape[2] - lc
    tail = lambda b: (b, 0, n // lc, 0)
    return pl.pallas_call(
        functools.partial(_ctxkv_kernel, mod_row=bsz),
        out_shape=(
            jax.ShapeDtypeStruct(k_all.shape, k_all.dtype),
            jax.ShapeDtypeStruct(v_all.shape, v_all.dtype),
        ),
        grid=(bsz,),
        in_specs=[
            pl.BlockSpec((1, lc, d), lambda b: (b, 0, 0)),
            _mod_spec(0),
            pl.BlockSpec((1, d), lambda b: (0, 0)),
            pl.BlockSpec((d, KV_W), lambda b: (0, 0)),
            pl.BlockSpec((d, KV_W), lambda b: (0, 1)),
            pl.BlockSpec((1, HEAD_DIM), lambda b: (0, 0)),
            pl.BlockSpec(memory_space=pl.ANY),
            pl.BlockSpec(memory_space=pl.ANY),
        ],
        out_specs=(
            pl.BlockSpec((1, N_KV_HEADS, lc, HEAD_DIM), tail),
            pl.BlockSpec((1, N_KV_HEADS, 1, V_AUG, lc), lambda b: (b, 0, n // ATTN_TK, 0, (n % ATTN_TK) // lc)),
        ),
        input_output_aliases={6: 0, 7: 1},
        compiler_params=_cparams("parallel"),
        name="ctxkv",
    )(ctx, mods, g, wqk, w, kg, k_all, v_all)


def _attn_kernel(q_ref, qn_ref, k_ref, vt_ref, za_ref, o_ref, s_ref, acc_ref, m_ref, *, tk):
    i = pl.program_id(2)
    n_chunks = k_ref.shape[2] // tk
    assert n_chunks % 2 == 1
    q = q_ref[0, 0]

    def scores(qt, c):
        k0 = pl.multiple_of(c * tk, tk)
        kb = k_ref[0, 0, pl.ds(k0, tk), :]
        return lax.dot_general(kb, qt, (((1,), (1,)), ((), ())), preferred_element_type=F32)

    def softmax_pv(s, c):
        m_prev = m_ref[...]
        m_new = jnp.maximum(m_prev, jnp.max(s, axis=0, keepdims=True))
        alpha = jnp.exp2(m_prev - m_new)
        p = jnp.exp2(s - m_new[0:1]).astype(BF16)
        pv = _dot(vt_ref[0, 0, c], p)
        acc_ref[...] = alpha[0:1] * acc_ref[...] + pv
        m_ref[...] = m_new

    @pl.when(i == 0)
    def _():
        s_ref[0] = scores(q, 0)

    m_ref[...] = jnp.full(m_ref.shape, -jnp.inf, F32)
    acc_ref[...] = jnp.zeros(acc_ref.shape, F32)

    def sweep(cur):
        def chunk(c, slot, next_scores):
            s = s_ref[slot]
            s_ref[1 - slot] = next_scores()
            softmax_pv(s, c)

        def body(t, carry):
            for u in range(ATTN_UNROLL):
                c = ATTN_UNROLL * t + u
                chunk(c, (cur + u) % 2, functools.partial(scores, q, c + 1))
            return carry

        n_loop = (n_chunks - 1) // ATTN_UNROLL
        lax.fori_loop(0, n_loop, body, 0)
        for c in range(n_loop * ATTN_UNROLL, n_chunks - 1):
            chunk(c, (cur + c) % 2, functools.partial(scores, q, c + 1))
        chunk(n_chunks - 1, cur, functools.partial(scores, qn_ref[0, 0], 0))

    for par in range(2):
        pl.when(i % 2 == par)(functools.partial(sweep, par))

    acc = acc_ref[...]
    o = (acc[0:HEAD_DIM] / acc[HEAD_DIM:HEAD_DIM + 1]).T
    o_ref[0] = (o * za_ref[0].astype(F32)).astype(BF16)


def _attention(q, k, vt, za, tq=512):
    bsz, nh, n, _ = q.shape
    nk = k.shape[2]
    n_chunks, _, tk = vt.shape[2:]
    assert nk == n_chunks * tk and n % tq == 0
    nq = n // tq
    return pl.pallas_call(
        functools.partial(_attn_kernel, tk=tk),
        out_shape=jax.ShapeDtypeStruct((bsz, n, ATTN_W), BF16),
        grid=(bsz, nh, nq),
        in_specs=[
            pl.BlockSpec((1, 1, tq, HEAD_DIM), lambda b, h, i: (b, h, i, 0)),
            pl.BlockSpec((1, 1, tq, HEAD_DIM), lambda b, h, i: (b, h, jnp.minimum(i + 1, nq - 1), 0)),
            pl.BlockSpec((1, 1, nk, HEAD_DIM), lambda b, h, i: (b, h // Q_PER_KV, 0, 0)),
            pl.BlockSpec((1, 1, n_chunks, V_AUG, tk), lambda b, h, i: (b, h // Q_PER_KV, 0, 0, 0)),
            pl.BlockSpec((1, tq, HEAD_DIM), lambda b, h, i: (b, i, h)),
        ],
        out_specs=pl.BlockSpec((1, tq, HEAD_DIM), lambda b, h, i: (b, i, h)),
        scratch_shapes=[
            pltpu.VMEM((2, tk, tq), F32),
            pltpu.VMEM((V_AUG, tq), F32),
            pltpu.VMEM((SUBLANES, tq), F32),
        ],
        compiler_params=_cparams("parallel", "parallel", "arbitrary"),
        name="attn",
    )(q, q, k, vt, za)


CONV_RB = 128
CONV_NC = CONV_CH // LANES


def _conv_kernel(yp_ref, yc_ref, yn_ref, zb_ref, w_ref, b_ref, lg_ref, lb_ref, o_ref,
                 win_ref, cv_ref, *, tm):
    i = pl.program_id(1)
    last = pl.num_programs(1) - 1
    prev = jnp.where(i > 0, yp_ref[0].astype(F32), 0.0)
    cur = yc_ref[0].astype(F32)
    nxt = jnp.where(i < last, yn_ref[0].astype(F32), 0.0)
    for c in range(CONV_NC):
        lanes = slice(c * LANES, (c + 1) * LANES)
        win_ref[c, 0:CONV_HALO] = prev[:, lanes]
        win_ref[c, CONV_HALO:CONV_HALO + tm] = cur[:, lanes]
        win_ref[c, CONV_HALO + tm:2 * CONV_HALO + tm] = nxt[:, lanes]

    off = CONV_HALO - CONV_WIDTH // 2

    def block(idx, carry):
        c = idx % CONV_NC
        base = pl.multiple_of((idx // CONV_NC) * CONV_RB, CONV_RB)
        out = None
        for shift in range(SUBLANES):
            part = None
            for j in range(CONV_WIDTH):
                if (off + j) % SUBLANES != shift:
                    continue
                rows = pl.ds(base + (off + j - shift), CONV_RB + SUBLANES)
                term = win_ref[c, rows, :] * w_ref[c, j:j + 1, :]
                part = term if part is None else part + term
            part = part[shift:shift + CONV_RB]
            out = part if out is None else out + part
        cv_ref[c, pl.ds(base, CONV_RB), :] = out + b_ref[c]
        return carry

    lax.fori_loop(0, (tm // CONV_RB) * CONV_NC, block, 0)

    cv = jnp.concatenate([cv_ref[c] for c in range(CONV_NC)], axis=1)
    mu = jnp.mean(cv, axis=-1, keepdims=True)
    cen = cv - mu
    var = jnp.mean(cen * cen, axis=-1, keepdims=True)
    yln = cen * lax.rsqrt(var + EPS) * lg_ref[...] + lb_ref[...]
    o_ref[0] = (_silu(yln) * zb_ref[0].astype(F32)).astype(BF16)


def _conv(y, zb, dw_w, dw_b, ln_g, ln_b, tm=512):
    bsz, n, ch = y.shape
    hb = tm // CONV_HALO
    nh = n // CONV_HALO
    row = lambda b, i: (b, i, 0)
    w3 = dw_w.reshape(CONV_WIDTH, CONV_NC, LANES).transpose(1, 0, 2)
    b3 = dw_b.reshape(CONV_NC, 1, LANES)
    return pl.pallas_call(
        functools.partial(_conv_kernel, tm=tm),
        out_shape=jax.ShapeDtypeStruct((bsz, n, ch), BF16),
        grid=(bsz, n // tm),
        in_specs=[
            pl.BlockSpec((1, CONV_HALO, ch), lambda b, i: (b, jnp.maximum(i * hb - 1, 0), 0)),
            pl.BlockSpec((1, tm, ch), row),
            pl.BlockSpec((1, CONV_HALO, ch), lambda b, i: (b, jnp.minimum((i + 1) * hb, nh - 1), 0)),
            pl.BlockSpec((1, tm, ch), row),
            pl.BlockSpec((CONV_NC, CONV_WIDTH, LANES), lambda b, i: (0, 0, 0)),
            pl.BlockSpec((CONV_NC, 1, LANES), lambda b, i: (0, 0, 0)),
            pl.BlockSpec((1, ch), lambda b, i: (0, 0)),
            pl.BlockSpec((1, ch), lambda b, i: (0, 0)),
        ],
        out_specs=pl.BlockSpec((1, tm, ch), row),
        scratch_shapes=[
            pltpu.VMEM((CONV_NC, tm + 2 * CONV_HALO, LANES), F32),
            pltpu.VMEM((CONV_NC, tm, LANES), F32),
        ],
        compiler_params=_cparams("parallel", "parallel"),
        name="conv",
    )(y, y, y, zb, w3, b3, ln_g, ln_b)


def _outproj_kernel(*refs, k_sizes, tn, final_norm):
    parts = refs[:len(k_sizes)]
    rest = refs[len(k_sizes):]
    if final_norm:
        w_ref, x_ref, mod_ref, fg_ref, o_ref = rest
    else:
        w_ref, x_ref, mod_ref, o_ref = rest
    d = D_MODEL
    gate = _mod_row(mod_ref, pl.program_id(0))[:, 2 * d:3 * d]
    ss = None
    for n in range(d // tn):
        cols = slice(n * tn, (n + 1) * tn)
        y = None
        k0 = 0
        for p_ref, ks in zip(parts, k_sizes):
            t = _dot(p_ref[0], w_ref[k0:k0 + ks, cols])
            y = t if y is None else y + t
            k0 += ks
        xn = x_ref[0, :, cols] + gate[:, cols] * y
        o_ref[0, :, cols] = xn
        if final_norm:
            sq = jnp.sum(xn * xn, axis=-1, keepdims=True)
            ss = sq if ss is None else ss + sq
    if final_norm:
        o_ref[0] = o_ref[0] * lax.rsqrt(ss * (1.0 / d) + EPS) * fg_ref[...]


def _outproj(parts, w, x, mods, layer, final_g=None, tm=512, tn=512):
    bsz, n, d = x.shape
    k_sizes = tuple(p.shape[-1] for p in parts)
    row = lambda b, i: (b, i, 0)
    in_specs = [pl.BlockSpec((1, tm, ks), row) for ks in k_sizes]
    in_specs += [
        _resident(w.shape),
        pl.BlockSpec((1, tm, d), row),
        _mod_spec(layer),
    ]
    args = list(parts) + [w, x, mods]
    if final_g is not None:
        in_specs.append(_resident((1, d)))
        args.append(final_g)
    return pl.pallas_call(
        functools.partial(_outproj_kernel, k_sizes=k_sizes, tn=tn, final_norm=final_g is not None),
        out_shape=jax.ShapeDtypeStruct((bsz, n, d), F32),
        grid=(bsz, n // tm),
        in_specs=in_specs,
        out_specs=pl.BlockSpec((1, tm, d), row),
        compiler_params=_cparams("parallel", "parallel"),
        name="outproj_final" if final_g is not None else "outproj",
    )(*args)


def _sgu_kernel(x_ref, mod_ref, g_ref, w_ref, lng_ref, lnb_ref, ws_ref, bsb_ref, o_ref,
                h_ref, v_ref, vb_ref, uz_ref, *, tm):
    d = D_MODEL
    mod = _mod_row(mod_ref, pl.program_id(0))
    h_ref[...] = _norm_mod(x_ref[0], g_ref[...], mod[:, 0:d], mod[:, d:2 * d]).astype(BF16)
    h = h_ref[...]

    seg = 512
    for s in range(SGU_W // seg):
        cols = slice(s * seg, (s + 1) * seg)
        v_ref[:, cols] = _gelu_tanh(_dot(h, w_ref[:, SGU_W + s * seg:SGU_W + (s + 1) * seg]))
    for g in range(SGU_GROUPS):
        cols = slice(g * SGU_GW, (g + 1) * SGU_GW)
        u = _gelu_tanh(_dot(h, w_ref[:, g * SGU_GW:(g + 1) * SGU_GW]))
        z = _silu(_dot(h, w_ref[:, 2 * SGU_W + g * SGU_GW:2 * SGU_W + (g + 1) * SGU_GW]))
        uz_ref[:, cols] = u * z

    v = v_ref[...]
    mu = jnp.mean(v, axis=-1, keepdims=True)
    cen = v - mu
    var = jnp.mean(cen * cen, axis=-1, keepdims=True)
    vb_ref[...] = (cen * lax.rsqrt(var + EPS) * lng_ref[...] + lnb_ref[...]).astype(BF16)

    for g in range(SGU_GROUPS):
        cols = slice(g * SGU_GW, (g + 1) * SGU_GW)
        for c in range(tm // CHUNK):
            rws = slice(c * CHUNK, (c + 1) * CHUNK)
            mixed = _dot(ws_ref[g], vb_ref[rws, cols]) + bsb_ref[:, cols]
            o_ref[0, rws, cols] = (uz_ref[rws, cols] * mixed).astype(BF16)


def _sgu(x, mods, g, w, ln_g, ln_b, ws, bsb, tm=256):
    bsz, n, d = x.shape
    row = lambda b, i: (b, i, 0)
    return pl.pallas_call(
        functools.partial(_sgu_kernel, tm=tm),
        out_shape=jax.ShapeDtypeStruct((bsz, n, SGU_W), BF16),
        grid=(bsz, n // tm),
        in_specs=[
            pl.BlockSpec((1, tm, d), row),
            _mod_spec(1),
            _resident((1, d)),
            _resident((d, OD_IN)),
            _resident((1, SGU_W)),
            _resident((1, SGU_W)),
            _resident((SGU_GROUPS, CHUNK, CHUNK)),
            _resident((CHUNK, SGU_W)),
        ],
        out_specs=pl.BlockSpec((1, tm, SGU_W), row),
        scratch_shapes=[
            pltpu.VMEM((tm, d), BF16),
            pltpu.VMEM((tm, SGU_W), F32),
            pltpu.VMEM((tm, SGU_W), BF16),
            pltpu.VMEM((tm, SGU_W), F32),
        ],
        compiler_params=_cparams("parallel", "parallel"),
        name="sgu",
    )(x, mods, g, w, ln_g, ln_b, ws, bsb)


def _deinterleave_heads(wc, n_heads):
    lead = wc.shape[:-1]
    t = wc.reshape(lead + (n_heads, HEAD_DIM // 2, 2))
    return jnp.swapaxes(t, -1, -2).reshape(lead + (n_heads * HEAD_DIM,))


def _rope_tables(n):
    rows = n // GRID_W
    row = np.repeat(np.arange(rows, dtype=np.float64), GRID_W)
    col = np.tile(np.arange(GRID_W, dtype=np.float64), rows)
    inv = np.power(ROPE_THETA, np.arange(N_FREQ, dtype=np.float64) * (-2.0 / AXIS_DIM))
    ang = np.concatenate([row[:, None] * inv, col[:, None] * inv], axis=-1)
    cos, sin = np.cos(ang), np.sin(ang)
    cs = np.concatenate([cos, cos], axis=-1).astype(np.float32)
    sn = np.concatenate([-sin, sin], axis=-1).astype(np.float32)
    return jnp.asarray(cs), jnp.asarray(sn)


def kernel(x, c, ctx, c_ctx, ada_w, ada_b, norm_g, ev_w_in, ev_q_norm, ev_k_norm, ev_dw_w, ev_dw_b,
           ev_ln_g, ev_ln_b, ev_w_out, od_w_in, od_ln_g, od_ln_b, od_ws, od_bs, od_w_out, final_g):
    bsz, n, d = x.shape
    assert bsz + 1 == MOD_ROWS

    mods = _mods(c, c_ctx, ada_w, ada_b)

    w_in = ev_w_in[0]
    w0 = w_in.astype(BF16)
    wqk = jnp.concatenate([
        _deinterleave_heads(w_in[:, :KV_W], N_KV_HEADS),
        _deinterleave_heads(w_in[:, 2 * KV_W:2 * KV_W + ATTN_W], N_Q_HEADS),
    ], axis=1).astype(BF16)
    qg = _deinterleave_heads(ev_q_norm[0], 1)[None, :]
    kg = _deinterleave_heads(ev_k_norm[0], 1)[None, :]
    cs, sn = _rope_tables(n)
    g0 = norm_g[0][None, :]
    g1 = norm_g[1][None, :]

    q, k, v, za, y, zb = _inproj0(x, mods, g0, w0, wqk, cs, sn, qg, kg, ctx.shape[1])
    k, v = _ctxkv(ctx, mods, g0, w0, wqk, kg, k, v)
    attn = _attention(q, k, v, za)
    conv = _conv(y, zb, ev_dw_w[0], ev_dw_b[0], ev_ln_g[0][None, :], ev_ln_b[0][None, :])
    x1 = _outproj([attn, conv], ev_w_out[0].astype(BF16), x, mods, 0)

    bsb = jnp.repeat(od_bs[0].T, SGU_GW, axis=1)
    mixed = _sgu(x1, mods, g1, od_w_in[0].astype(BF16), od_ln_g[0][None, :], od_ln_b[0][None, :],
                 od_ws[0].astype(BF16), bsb)
    return _outproj([mixed], od_w_out[0].astype(BF16), x1, mods, 1, final_g=final_g[None, :])
```

```python
import functools

import jax
import jax.numpy as jnp
import numpy as np
from jax import lax
from jax.experimental import pallas as pl
from jax.experimental.pallas import tpu as pltpu

D_MODEL = 2048
GRID_W = 64
HEAD_DIM = 128
N_Q_HEADS = 8
N_KV_HEADS = 2
Q_PER_KV = N_Q_HEADS // N_KV_HEADS
ATTN_W = N_Q_HEADS * HEAD_DIM
KV_W = N_KV_HEADS * HEAD_DIM
ATTN_SCALE = HEAD_DIM ** -0.5
Q_SCALE = ATTN_SCALE * 1.4426950408889634
V_AUG = HEAD_DIM + 16
ATTN_TK = 768
ATTN_UNROLL = 4
ROPE_THETA = 10000.0
AXIS_DIM = HEAD_DIM // 2
N_FREQ = AXIS_DIM // 2
CONV_CH = 1024
CONV_WIDTH = 31
CONV_HALO = 16
CHUNK = 128
SGU_W = D_MODEL
SGU_GROUPS = 8
SGU_GW = SGU_W // SGU_GROUPS
EV_IN = 2 * KV_W + 2 * ATTN_W + 3 * CONV_CH
OD_IN = 3 * SGU_W
EPS = 1e-6

LANES = 128
SUBLANES = 8
VMEM_LIMIT = 56 * 1024 * 1024

F32 = jnp.float32
BF16 = jnp.bfloat16


def _cparams(*sem):
    return pltpu.CompilerParams(dimension_semantics=sem, vmem_limit_bytes=VMEM_LIMIT)


def _resident(shape):
    nd = len(shape)
    return pl.BlockSpec(shape, lambda *_: (0,) * nd, pipeline_mode=pl.Buffered(1))


def _sigmoid(x):
    return 1.0 / (1.0 + jnp.exp(-x))


def _silu(x):
    return x * _sigmoid(x)


def _gelu_tanh(x):
    return 0.5 * x * (1.0 + jnp.tanh(0.7978845608028654 * (x + 0.044715 * (x * x * x))))


def _dot(a, b):
    return jnp.dot(a, b, preferred_element_type=F32)


def _norm_mod(x, g, shift, scale):
    ms = jnp.mean(x * x, axis=-1, keepdims=True)
    return (x * lax.rsqrt(ms + EPS) * g) * (1.0 + scale) + shift


def _mod_spec(layer):
    return pl.BlockSpec((1, SUBLANES, 3 * D_MODEL), lambda *_: (layer, 0, 0))


def _mod_row(mod_ref, row):
    return mod_ref[0, pl.ds(row, 1), :]


MOD_ROWS = 3
MOD_KC = 64


def _mods_kernel(cb_ref, w_ref, b_ref, o_ref, sc_ref, *, tn):
    reps = tn // LANES

    @pl.when(jnp.logical_and(pl.program_id(0) == 0, pl.program_id(1) == 0))
    def _():
        sc_ref[...] = _silu(cb_ref[...])

    def body(kc, accs):
        k0 = pl.multiple_of(kc * MOD_KC, MOD_KC)
        w = w_ref[0, pl.ds(k0, MOD_KC), :]
        out = []
        for r in range(MOD_ROWS):
            s = sc_ref[r, pl.ds(k0, MOD_KC), :]
            prod = jnp.concatenate([s] * reps, axis=1) * w
            out.append(accs[r] + prod.reshape(MOD_KC // SUBLANES, SUBLANES, tn).sum(axis=0))
        return tuple(out)

    zero = jnp.zeros((SUBLANES, tn), F32)
    accs = lax.fori_loop(0, D_MODEL // MOD_KC, body, (zero,) * MOD_ROWS)
    rows = [jnp.sum(a, axis=0, keepdims=True) + b_ref[0] for a in accs]
    rows.append(jnp.zeros((SUBLANES - MOD_ROWS, tn), F32))
    o_ref[0] = jnp.concatenate(rows, axis=0)


def _mods(c, c_ctx, ada_w, ada_b, tn=1024):
    depth, d, n = ada_w.shape
    cc = jnp.concatenate([c, c_ctx[None, :]], axis=0)
    cb = jnp.broadcast_to(cc[:, :, None], (MOD_ROWS, d, LANES))
    return pl.pallas_call(
        functools.partial(_mods_kernel, tn=tn),
        out_shape=jax.ShapeDtypeStruct((depth, SUBLANES, n), F32),
        grid=(depth, n // tn),
        in_specs=[
            pl.BlockSpec((MOD_ROWS, d, LANES), lambda l, j: (0, 0, 0)),
            pl.BlockSpec((1, d, tn), lambda l, j: (l, 0, j)),
            pl.BlockSpec((1, 1, tn), lambda l, j: (l, 0, j)),
        ],
        out_specs=pl.BlockSpec((1, SUBLANES, tn), lambda l, j: (l, 0, j)),
        scratch_shapes=[pltpu.VMEM((MOD_ROWS, d, LANES), F32)],
        compiler_params=_cparams("arbitrary", "arbitrary"),
        name="mods",
    )(cb, ada_w, ada_b.reshape(depth, 1, n))


def _head_norm_rope(t, gain, cs, sn):
    ms = jnp.mean(t * t, axis=-1, keepdims=True)
    tn = t * lax.rsqrt(ms + EPS) * gain
    return tn * cs + pltpu.roll(tn, HEAD_DIM // 2, axis=1) * sn


def _inproj0_kernel(x_ref, mod_ref, g_ref, w_ref, wqk_ref, cs_ref, sn_ref, qg_ref, kg_ref,
                    q_ref, k_ref, v_ref, za_ref, y_ref, zb_ref, h_ref):
    d = D_MODEL
    mod = _mod_row(mod_ref, pl.program_id(0))
    h_ref[...] = _norm_mod(x_ref[0], g_ref[...], mod[:, 0:d], mod[:, d:2 * d]).astype(BF16)
    h = h_ref[...]
    cs = cs_ref[...]
    sn = sn_ref[...]

    kk = _dot(h, wqk_ref[:, 0:KV_W])
    vv = _dot(h, w_ref[:, KV_W:2 * KV_W])
    ones = jnp.ones((V_AUG - HEAD_DIM, kk.shape[0]), BF16)
    for j in range(N_KV_HEADS):
        kj = kk[:, j * HEAD_DIM:(j + 1) * HEAD_DIM]
        k_ref[0, j] = _head_norm_rope(kj, kg_ref[...], cs, sn).astype(BF16)
        vj = vv[:, j * HEAD_DIM:(j + 1) * HEAD_DIM]
        v_ref[0, j, 0, 0:HEAD_DIM, :] = vj.T.astype(BF16)
        v_ref[0, j, 0, HEAD_DIM:V_AUG, :] = ones

    seg = 512
    heads_per_seg = seg // HEAD_DIM
    for s in range(ATTN_W // seg):
        qq = _dot(h, wqk_ref[:, KV_W + s * seg:KV_W + (s + 1) * seg])
        for j in range(heads_per_seg):
            qj = qq[:, j * HEAD_DIM:(j + 1) * HEAD_DIM]
            qj = _head_norm_rope(qj, qg_ref[...], cs, sn) * Q_SCALE
            q_ref[0, s * heads_per_seg + j] = qj.astype(BF16)

    base = 2 * KV_W + ATTN_W
    for s in range(ATTN_W // seg):
        z = _dot(h, w_ref[:, base + s * seg:base + (s + 1) * seg])
        za_ref[0, :, s * seg:(s + 1) * seg] = _silu(z).astype(BF16)

    base += ATTN_W
    for s in range(CONV_CH // seg):
        a = _dot(h, w_ref[:, base + s * seg:base + (s + 1) * seg])
        b = _dot(h, w_ref[:, base + CONV_CH + s * seg:base + CONV_CH + (s + 1) * seg])
        y_ref[0, :, s * seg:(s + 1) * seg] = (a * _sigmoid(b)).astype(BF16)

    base += 2 * CONV_CH
    for s in range(CONV_CH // seg):
        z = _dot(h, w_ref[:, base + s * seg:base + (s + 1) * seg])
        zb_ref[0, :, s * seg:(s + 1) * seg] = _silu(z).astype(BF16)


def _inproj0(x, mods, g, w, wqk, cs, sn, qg, kg, n_ctx, tm=256):
    bsz, n, d = x.shape
    row = lambda b, i: (b, i, 0)
    head = lambda b, i: (b, 0, i, 0)
    tpc = ATTN_TK // tm
    return pl.pallas_call(
        _inproj0_kernel,
        out_shape=(
            jax.ShapeDtypeStruct((bsz, N_Q_HEADS, n, HEAD_DIM), BF16),
            jax.ShapeDtypeStruct((bsz, N_KV_HEADS, n + n_ctx, HEAD_DIM), BF16),
            jax.ShapeDtypeStruct((bsz, N_KV_HEADS, (n + n_ctx) // ATTN_TK, V_AUG, ATTN_TK), BF16),
            jax.ShapeDtypeStruct((bsz, n, ATTN_W), BF16),
            jax.ShapeDtypeStruct((bsz, n, CONV_CH), BF16),
            jax.ShapeDtypeStruct((bsz, n, CONV_CH), BF16),
        ),
        grid=(bsz, n // tm),
        in_specs=[
            pl.BlockSpec((1, tm, d), row),
            _mod_spec(0),
            _resident((1, d)),
            _resident((d, EV_IN)),
            _resident((d, KV_W + ATTN_W)),
            pl.BlockSpec((tm, HEAD_DIM), lambda b, i: (i, 0)),
            pl.BlockSpec((tm, HEAD_DIM), lambda b, i: (i, 0)),
            _resident((1, HEAD_DIM)),
            _resident((1, HEAD_DIM)),
        ],
        out_specs=(
            pl.BlockSpec((1, N_Q_HEADS, tm, HEAD_DIM), head),
            pl.BlockSpec((1, N_KV_HEADS, tm, HEAD_DIM), head),
            pl.BlockSpec((1, N_KV_HEADS, 1, V_AUG, tm), lambda b, i: (b, 0, i // tpc, 0, i % tpc)),
            pl.BlockSpec((1, tm, ATTN_W), row),
            pl.BlockSpec((1, tm, CONV_CH), row),
            pl.BlockSpec((1, tm, CONV_CH), row),
        ),
        scratch_shapes=[pltpu.VMEM((tm, d), BF16)],
        compiler_params=_cparams("parallel", "parallel"),
        name="inproj0",
    )(x, mods, g, w, wqk, cs, sn, qg, kg)


def _ctxkv_kernel(x_ref, mod_ref, g_ref, wk_ref, wv_ref, kg_ref, k_in, v_in, k_ref, v_ref, *, mod_row):
    del k_in, v_in
    d = D_MODEL
    mod = _mod_row(mod_ref, mod_row)
    h = _norm_mod(x_ref[0], g_ref[...], mod[:, 0:d], mod[:, d:2 * d]).astype(BF16)
    kk = _dot(h, wk_ref[...])
    vv = _dot(h, wv_ref[...])
    ones = jnp.ones((V_AUG - HEAD_DIM, kk.shape[0]), BF16)
    for j in range(N_KV_HEADS):
        kj = kk[:, j * HEAD_DIM:(j + 1) * HEAD_DIM]
        ms = jnp.mean(kj * kj, axis=-1, keepdims=True)
        k_ref[0, j] = (kj * lax.rsqrt(ms + EPS) * kg_ref[...]).astype(BF16)
        vj = vv[:, j * HEAD_DIM:(j + 1) * HEAD_DIM]
        v_ref[0, j, 0, 0:HEAD_DIM, :] = vj.T.astype(BF16)
        v_ref[0, j, 0, HEAD_DIM:V_AUG, :] = ones


def _ctxkv(ctx, mods, g, w, wqk, kg, k_all, v_all):
    bsz, lc, d = ctx.shape
    n = k_all.shape[2] - lc
    tail = lambda b: (b, 0, n // lc, 0)
    return pl.pallas_call(
        functools.partial(_ctxkv_kernel, mod_row=bsz),
        out_shape=(
            jax.ShapeDtypeStruct(k_all.shape, k_all.dtype),
            jax.ShapeDtypeStruct(v_all.shape, v_all.dtype),
        ),
        grid=(bsz,),
        in_specs=[
            pl.BlockSpec((1, lc, d), lambda b: (b, 0, 0)),
            _mod_spec(0),
            pl.BlockSpec((1, d), lambda b: (0, 0)),
            pl.BlockSpec((d, KV_W), lambda b: (0, 0)),
            pl.BlockSpec((d, KV_W), lambda b: (0, 1)),
            pl.BlockSpec((1, HEAD_DIM), lambda b: (0, 0)),
            pl.BlockSpec(memory_space=pl.ANY),
            pl.BlockSpec(memory_space=pl.ANY),
        ],
        out_specs=(
            pl.BlockSpec((1, N_KV_HEADS, lc, HEAD_DIM), tail),
            pl.BlockSpec((1, N_KV_HEADS, 1, V_AUG, lc), lambda b: (b, 0, n // ATTN_TK, 0, (n % ATTN_TK) // lc)),
        ),
        input_output_aliases={6: 0, 7: 1},
        compiler_params=_cparams("parallel"),
        name="ctxkv",
    )(ctx, mods, g, wqk, w, kg, k_all, v_all)


def _attn_kernel(q_ref, qn_ref, k_ref, vt_ref, za_ref, o_ref, s_ref, acc_ref, m_ref, *, tk):
    i = pl.program_id(2)
    n_chunks = k_ref.shape[2] // tk
    assert n_chunks % 2 == 1
    q = q_ref[0, 0]

    def scores(qt, c):
        k0 = pl.multiple_of(c * tk, tk)
        kb = k_ref[0, 0, pl.ds(k0, tk), :]
        return lax.dot_general(kb, qt, (((1,), (1,)), ((), ())), preferred_element_type=F32)

    def softmax_pv(s, c):
        m_prev = m_ref[...]
        m_new = jnp.maximum(m_prev, jnp.max(s, axis=0, keepdims=True))
        alpha = jnp.exp2(m_prev - m_new)
        p = jnp.exp2(s - m_new[0:1]).astype(BF16)
        pv = _dot(vt_ref[0, 0, c], p)
        acc_ref[...] = alpha[0:1] * acc_ref[...] + pv
        m_ref[...] = m_new

    @pl.when(i == 0)
    def _():
        s_ref[0] = scores(q, 0)

    m_ref[...] = jnp.full(m_ref.shape, -jnp.inf, F32)
    acc_ref[...] = jnp.zeros(acc_ref.shape, F32)

    def sweep(cur):
        def chunk(c, slot, next_scores):
            s = s_ref[slot]
            s_ref[1 - slot] = next_scores()
            softmax_pv(s, c)

        def body(t, carry):
            for u in range(ATTN_UNROLL):
                c = ATTN_UNROLL * t + u
                chunk(c, (cur + u) % 2, functools.partial(scores, q, c + 1))
            return carry

        n_loop = (n_chunks - 1) // ATTN_UNROLL
        lax.fori_loop(0, n_loop, body, 0)
        for c in range(n_loop * ATTN_UNROLL, n_chunks - 1):
            chunk(c, (cur + c) % 2, functools.partial(scores, q, c + 1))
        chunk(n_chunks - 1, cur, functools.partial(scores, qn_ref[0, 0], 0))

    for par in range(2):
        pl.when(i % 2 == par)(functools.partial(sweep, par))

    acc = acc_ref[...]
    o = (acc[0:HEAD_DIM] / acc[HEAD_DIM:HEAD_DIM + 1]).T
    o_ref[0] = (o * za_ref[0].astype(F32)).astype(BF16)


def _attention(q, k, vt, za, tq=512):
    bsz, nh, n, _ = q.shape
    nk = k.shape[2]
    n_chunks, _, tk = vt.shape[2:]
    assert nk == n_chunks * tk and n % tq == 0
    nq = n // tq
    return pl.pallas_call(
        functools.partial(_attn_kernel, tk=tk),
        out_shape=jax.ShapeDtypeStruct((bsz, n, ATTN_W), BF16),
        grid=(bsz, nh, nq),
        in_specs=[
            pl.BlockSpec((1, 1, tq, HEAD_DIM), lambda b, h, i: (b, h, i, 0)),
            pl.BlockSpec((1, 1, tq, HEAD_DIM), lambda b, h, i: (b, h, jnp.minimum(i + 1, nq - 1), 0)),
            pl.BlockSpec((1, 1, nk, HEAD_DIM), lambda b, h, i: (b, h // Q_PER_KV, 0, 0)),
            pl.BlockSpec((1, 1, n_chunks, V_AUG, tk), lambda b, h, i: (b, h // Q_PER_KV, 0, 0, 0)),
            pl.BlockSpec((1, tq, HEAD_DIM), lambda b, h, i: (b, i, h)),
        ],
        out_specs=pl.BlockSpec((1, tq, HEAD_DIM), lambda b, h, i: (b, i, h)),
        scratch_shapes=[
            pltpu.VMEM((2, tk, tq), F32),
            pltpu.VMEM((V_AUG, tq), F32),
            pltpu.VMEM((SUBLANES, tq), F32),
        ],
        compiler_params=_cparams("parallel", "parallel", "arbitrary"),
        name="attn",
    )(q, q, k, vt, za)


CONV_RB = 128
CONV_NC = CONV_CH // LANES


def _conv_kernel(yp_ref, yc_ref, yn_ref, w_ref, b_ref, o_ref, win_ref, *, tm):
    i = pl.program_id(1)
    last = pl.num_programs(1) - 1
    prev = jnp.where(i > 0, yp_ref[0].astype(F32), 0.0)
    cur = yc_ref[0].astype(F32)
    nxt = jnp.where(i < last, yn_ref[0].astype(F32), 0.0)
    for c in range(CONV_NC):
        lanes = slice(c * LANES, (c + 1) * LANES)
        win_ref[c, 0:CONV_HALO] = prev[:, lanes]
        win_ref[c, CONV_HALO:CONV_HALO + tm] = cur[:, lanes]
        win_ref[c, CONV_HALO + tm:2 * CONV_HALO + tm] = nxt[:, lanes]

    off = CONV_HALO - CONV_WIDTH // 2

    def block(idx, carry):
        c = idx % CONV_NC
        base = pl.multiple_of((idx // CONV_NC) * CONV_RB, CONV_RB)
        out = None
        for shift in range(SUBLANES):
            part = None
            for j in range(CONV_WIDTH):
                if (off + j) % SUBLANES != shift:
                    continue
                rows = pl.ds(base + (off + j - shift), CONV_RB + SUBLANES)
                term = win_ref[c, rows, :] * w_ref[c, j:j + 1, :]
                part = term if part is None else part + term
            part = part[shift:shift + CONV_RB]
            out = part if out is None else out + part
        o_ref[0, c, pl.ds(base, CONV_RB), :] = out + b_ref[c]
        return carry

    lax.fori_loop(0, (tm // CONV_RB) * CONV_NC, block, 0)


def _conv(y, dw_w, dw_b, tm=512):
    bsz, n, ch = y.shape
    hb = tm // CONV_HALO
    nh = n // CONV_HALO
    w3 = dw_w.reshape(CONV_WIDTH, CONV_NC, LANES).transpose(1, 0, 2)
    b3 = dw_b.reshape(CONV_NC, 1, LANES)
    return pl.pallas_call(
        functools.partial(_conv_kernel, tm=tm),
        out_shape=jax.ShapeDtypeStruct((bsz, CONV_NC, n, LANES), F32),
        grid=(bsz, n // tm),
        in_specs=[
            pl.BlockSpec((1, CONV_HALO, ch), lambda b, i: (b, jnp.maximum(i * hb - 1, 0), 0)),
            pl.BlockSpec((1, tm, ch), lambda b, i: (b, i, 0)),
            pl.BlockSpec((1, CONV_HALO, ch), lambda b, i: (b, jnp.minimum((i + 1) * hb, nh - 1), 0)),
            pl.BlockSpec((CONV_NC, CONV_WIDTH, LANES), lambda b, i: (0, 0, 0)),
            pl.BlockSpec((CONV_NC, 1, LANES), lambda b, i: (0, 0, 0)),
        ],
        out_specs=pl.BlockSpec((1, CONV_NC, tm, LANES), lambda b, i: (b, 0, i, 0)),
        scratch_shapes=[pltpu.VMEM((CONV_NC, tm + 2 * CONV_HALO, LANES), F32)],
        compiler_params=_cparams("parallel", "parallel"),
        name="conv",
    )(y, y, y, w3, b3)


def _outproj_kernel(*refs, k_sizes, tn, final_norm):
    parts = refs[:len(k_sizes)]
    rest = refs[len(k_sizes):]
    if final_norm:
        w_ref, x_ref, mod_ref, fg_ref, o_ref = rest
    else:
        w_ref, x_ref, mod_ref, o_ref = rest
    d = D_MODEL
    gate = _mod_row(mod_ref, pl.program_id(0))[:, 2 * d:3 * d]
    ss = None
    for n in range(d // tn):
        cols = slice(n * tn, (n + 1) * tn)
        y = None
        k0 = 0
        for p_ref, ks in zip(parts, k_sizes):
            t = _dot(p_ref[0], w_ref[k0:k0 + ks, cols])
            y = t if y is None else y + t
            k0 += ks
        xn = x_ref[0, :, cols] + gate[:, cols] * y
        o_ref[0, :, cols] = xn
        if final_norm:
            sq = jnp.sum(xn * xn, axis=-1, keepdims=True)
            ss = sq if ss is None else ss + sq
    if final_norm:
        o_ref[0] = o_ref[0] * lax.rsqrt(ss * (1.0 / d) + EPS) * fg_ref[...]


def _outproj0_kernel(attn_ref, cv_ref, zb_ref, lg_ref, lb_ref, w_ref, x_ref, mod_ref, o_ref, cg_ref, *, tn):
    d = D_MODEL
    cv = jnp.concatenate([cv_ref[0, c] for c in range(CONV_NC)], axis=1)
    mu = jnp.mean(cv, axis=-1, keepdims=True)
    cen = cv - mu
    var = jnp.mean(cen * cen, axis=-1, keepdims=True)
    yln = cen * lax.rsqrt(var + EPS) * lg_ref[...] + lb_ref[...]
    cg_ref[...] = (_silu(yln) * zb_ref[0].astype(F32)).astype(BF16)
    gate = _mod_row(mod_ref, pl.program_id(0))[:, 2 * d:3 * d]
    for n in range(d // tn):
        cols = slice(n * tn, (n + 1) * tn)
        y = _dot(attn_ref[0], w_ref[0:ATTN_W, cols]) + _dot(cg_ref[...], w_ref[ATTN_W:ATTN_W + CONV_CH, cols])
        o_ref[0, :, cols] = x_ref[0, :, cols] + gate[:, cols] * y


def _outproj0(attn, cv, zb, ln_g, ln_b, w, x, mods, tm=512, tn=512):
    bsz, n, d = x.shape
    row = lambda b, i: (b, i, 0)
    return pl.pallas_call(
        functools.partial(_outproj0_kernel, tn=tn),
        out_shape=jax.ShapeDtypeStruct((bsz, n, d), F32),
        grid=(bsz, n // tm),
        in_specs=[
            pl.BlockSpec((1, tm, ATTN_W), row),
            pl.BlockSpec((1, CONV_NC, tm, LANES), lambda b, i: (b, 0, i, 0)),
            pl.BlockSpec((1, tm, CONV_CH), row),
            _resident((1, CONV_CH)),
            _resident((1, CONV_CH)),
            _resident(w.shape),
            pl.BlockSpec((1, tm, d), row),
            _mod_spec(0),
        ],
        out_specs=pl.BlockSpec((1, tm, d), row),
        scratch_shapes=[pltpu.VMEM((tm, CONV_CH), BF16)],
        compiler_params=_cparams("parallel", "parallel"),
        name="outproj0",
    )(attn, cv, zb, ln_g, ln_b, w, x, mods)


def _outproj(parts, w, x, mods, layer, final_g=None, tm=512, tn=512):
    bsz, n, d = x.shape
    k_sizes = tuple(p.shape[-1] for p in parts)
    row = lambda b, i: (b, i, 0)
    in_specs = [pl.BlockSpec((1, tm, ks), row) for ks in k_sizes]
    in_specs += [
        _resident(w.shape),
        pl.BlockSpec((1, tm, d), row),
        _mod_spec(layer),
    ]
    args = list(parts) + [w, x, mods]
    if final_g is not None:
        in_specs.append(_resident((1, d)))
        args.append(final_g)
    return pl.pallas_call(
        functools.partial(_outproj_kernel, k_sizes=k_sizes, tn=tn, final_norm=final_g is not None),
        out_shape=jax.ShapeDtypeStruct((bsz, n, d), F32),
        grid=(bsz, n // tm),
        in_specs=in_specs,
        out_specs=pl.BlockSpec((1, tm, d), row),
        compiler_params=_cparams("parallel", "parallel"),
        name="outproj_final" if final_g is not None else "outproj",
    )(*args)


def _sgu_kernel(x_ref, mod_ref, g_ref, w_ref, lng_ref, lnb_ref, ws_ref, bsb_ref, o_ref,
                h_ref, v_ref, vb_ref, uz_ref, *, tm):
    d = D_MODEL
    mod = _mod_row(mod_ref, pl.program_id(0))
    h_ref[...] = _norm_mod(x_ref[0], g_ref[...], mod[:, 0:d], mod[:, d:2 * d]).astype(BF16)
    h = h_ref[...]

    seg = 512
    for s in range(SGU_W // seg):
        cols = slice(s * seg, (s + 1) * seg)
        v_ref[:, cols] = _gelu_tanh(_dot(h, w_ref[:, SGU_W + s * seg:SGU_W + (s + 1) * seg]))
    for g in range(SGU_GROUPS):
        cols = slice(g * SGU_GW, (g + 1) * SGU_GW)
        u = _gelu_tanh(_dot(h, w_ref[:, g * SGU_GW:(g + 1) * SGU_GW]))
        z = _silu(_dot(h, w_ref[:, 2 * SGU_W + g * SGU_GW:2 * SGU_W + (g + 1) * SGU_GW]))
        uz_ref[:, cols] = u * z

    v = v_ref[...]
    mu = jnp.mean(v, axis=-1, keepdims=True)
    cen = v - mu
    var = jnp.mean(cen * cen, axis=-1, keepdims=True)
    vb_ref[...] = (cen * lax.rsqrt(var + EPS) * lng_ref[...] + lnb_ref[...]).astype(BF16)

    for g in range(SGU_GROUPS):
        cols = slice(g * SGU_GW, (g + 1) * SGU_GW)
        for c in range(tm // CHUNK):
            rws = slice(c * CHUNK, (c + 1) * CHUNK)
            mixed = _dot(ws_ref[g], vb_ref[rws, cols]) + bsb_ref[:, cols]
            o_ref[0, rws, cols] = (uz_ref[rws, cols] * mixed).astype(BF16)


def _sgu(x, mods, g, w, ln_g, ln_b, ws, bsb, tm=256):
    bsz, n, d = x.shape
    row = lambda b, i: (b, i, 0)
    return pl.pallas_call(
        functools.partial(_sgu_kernel, tm=tm),
        out_shape=jax.ShapeDtypeStruct((bsz, n, SGU_W), BF16),
        grid=(bsz, n // tm),
        in_specs=[
            pl.BlockSpec((1, tm, d), row),
            _mod_spec(1),
            _resident((1, d)),
            _resident((d, OD_IN)),
            _resident((1, SGU_W)),
            _resident((1, SGU_W)),
            _resident((SGU_GROUPS, CHUNK, CHUNK)),
            _resident((CHUNK, SGU_W)),
        ],
        out_specs=pl.BlockSpec((1, tm, SGU_W), row),
        scratch_shapes=[
            pltpu.VMEM((tm, d), BF16),
            pltpu.VMEM((tm, SGU_W), F32),
            pltpu.VMEM((tm, SGU_W), BF16),
            pltpu.VMEM((tm, SGU_W), F32),
        ],
        compiler_params=_cparams("parallel", "parallel"),
        name="sgu",
    )(x, mods, g, w, ln_g, ln_b, ws, bsb)


def _deinterleave_heads(wc, n_heads):
    lead = wc.shape[:-1]
    t = wc.reshape(lead + (n_heads, HEAD_DIM // 2, 2))
    return jnp.swapaxes(t, -1, -2).reshape(lead + (n_heads * HEAD_DIM,))


def _rope_tables(n):
    rows = n // GRID_W
    row = np.repeat(np.arange(rows, dtype=np.float64), GRID_W)
    col = np.tile(np.arange(GRID_W, dtype=np.float64), rows)
    inv = np.power(ROPE_THETA, np.arange(N_FREQ, dtype=np.float64) * (-2.0 / AXIS_DIM))
    ang = np.concatenate([row[:, None] * inv, col[:, None] * inv], axis=-1)
    cos, sin = np.cos(ang), np.sin(ang)
    cs = np.concatenate([cos, cos], axis=-1).astype(np.float32)
    sn = np.concatenate([-sin, sin], axis=-1).astype(np.float32)
    return jnp.asarray(cs), jnp.asarray(sn)


def kernel(x, c, ctx, c_ctx, ada_w, ada_b, norm_g, ev_w_in, ev_q_norm, ev_k_norm, ev_dw_w, ev_dw_b,
           ev_ln_g, ev_ln_b, ev_w_out, od_w_in, od_ln_g, od_ln_b, od_ws, od_bs, od_w_out, final_g):
    bsz, n, d = x.shape
    assert bsz + 1 == MOD_ROWS

    mods = _mods(c, c_ctx, ada_w, ada_b)

    w_in = ev_w_in[0]
    w0 = w_in.astype(BF16)
    wqk = jnp.concatenate([
        _deinterleave_heads(w_in[:, :KV_W], N_KV_HEADS),
        _deinterleave_heads(w_in[:, 2 * KV_W:2 * KV_W + ATTN_W], N_Q_HEADS),
    ], axis=1).astype(BF16)
    qg = _deinterleave_heads(ev_q_norm[0], 1)[None, :]
    kg = _deinterleave_heads(ev_k_norm[0], 1)[None, :]
    cs, sn = _rope_tables(n)
    g0 = norm_g[0][None, :]
    g1 = norm_g[1][None, :]

    q, k, v, za, y, zb = _inproj0(x, mods, g0, w0, wqk, cs, sn, qg, kg, ctx.shape[1])
    k, v = _ctxkv(ctx, mods, g0, w0, wqk, kg, k, v)
    attn = _attention(q, k, v, za)
    cv = _conv(y, ev_dw_w[0], ev_dw_b[0])
    x1 = _outproj0(attn, cv, zb, ev_ln_g[0][None, :], ev_ln_b[0][None, :], ev_w_out[0].astype(BF16), x, mods)

    bsb = jnp.repeat(od_bs[0].T, SGU_GW, axis=1)
    mixed = _sgu(x1, mods, g1, od_w_in[0].astype(BF16), od_ln_g[0][None, :], od_ln_b[0][None, :],
                 od_ws[0].astype(BF16), bsb)
    return _outproj([mixed], od_w_out[0].astype(BF16), x1, mods, 1, final_g=final_g[None, :])
```

```python
import functools

import jax
import jax.numpy as jnp
import numpy as np
from jax import lax
from jax.experimental import pallas as pl
from jax.experimental.pallas import tpu as pltpu

D_MODEL = 2048
GRID_W = 64
HEAD_DIM = 128
N_Q_HEADS = 8
N_KV_HEADS = 2
Q_PER_KV = N_Q_HEADS // N_KV_HEADS
ATTN_W = N_Q_HEADS * HEAD_DIM
KV_W = N_KV_HEADS * HEAD_DIM
ATTN_SCALE = HEAD_DIM ** -0.5
Q_SCALE = ATTN_SCALE * 1.4426950408889634
V_AUG = HEAD_DIM + 16
ATTN_TK = 768
ATTN_UNROLL = 4
ROPE_THETA = 10000.0
AXIS_DIM = HEAD_DIM // 2
N_FREQ = AXIS_DIM // 2
CONV_CH = 1024
CONV_WIDTH = 31
CONV_HALO = 16
CHUNK = 128
SGU_W = D_MODEL
SGU_GROUPS = 8
SGU_GW = SGU_W // SGU_GROUPS
EV_IN = 2 * KV_W + 2 * ATTN_W + 3 * CONV_CH
OD_IN = 3 * SGU_W
EPS = 1e-6

LANES = 128
SUBLANES = 8
VMEM_LIMIT = 56 * 1024 * 1024

F32 = jnp.float32
BF16 = jnp.bfloat16


def _cparams(*sem):
    return pltpu.CompilerParams(dimension_semantics=sem, vmem_limit_bytes=VMEM_LIMIT)


def _resident(shape):
    nd = len(shape)
    return pl.BlockSpec(shape, lambda *_: (0,) * nd, pipeline_mode=pl.Buffered(1))


def _sigmoid(x):
    return 1.0 / (1.0 + jnp.exp(-x))


def _silu(x):
    return x * _sigmoid(x)


def _gelu_tanh(x):
    return 0.5 * x * (1.0 + jnp.tanh(0.7978845608028654 * (x + 0.044715 * (x * x * x))))


def _dot(a, b):
    return jnp.dot(a, b, preferred_element_type=F32)


def _norm_mod(x, g, shift, scale):
    ms = jnp.mean(x * x, axis=-1, keepdims=True)
    return (x * lax.rsqrt(ms + EPS) * g) * (1.0 + scale) + shift


def _mod_spec(layer):
    return pl.BlockSpec((1, SUBLANES, 3 * D_MODEL), lambda *_: (layer, 0, 0))


def _mod_row(mod_ref, row):
    return mod_ref[0, pl.ds(row, 1), :]


MOD_ROWS = 3
MOD_KC = 64


def _mods_kernel(cb_ref, w_ref, b_ref, o_ref, sc_ref, *, tn):
    reps = tn // LANES

    @pl.when(jnp.logical_and(pl.program_id(0) == 0, pl.program_id(1) == 0))
    def _():
        sc_ref[...] = _silu(cb_ref[...])

    def body(kc, accs):
        k0 = pl.multiple_of(kc * MOD_KC, MOD_KC)
        w = w_ref[0, pl.ds(k0, MOD_KC), :]
        out = []
        for r in range(MOD_ROWS):
            s = sc_ref[r, pl.ds(k0, MOD_KC), :]
            prod = jnp.concatenate([s] * reps, axis=1) * w
            out.append(accs[r] + prod.reshape(MOD_KC // SUBLANES, SUBLANES, tn).sum(axis=0))
        return tuple(out)

    zero = jnp.zeros((SUBLANES, tn), F32)
    accs = lax.fori_loop(0, D_MODEL // MOD_KC, body, (zero,) * MOD_ROWS)
    rows = [jnp.sum(a, axis=0, keepdims=True) + b_ref[0] for a in accs]
    rows.append(jnp.zeros((SUBLANES - MOD_ROWS, tn), F32))
    o_ref[0] = jnp.concatenate(rows, axis=0)


def _mods(c, c_ctx, ada_w, ada_b, tn=1024):
    depth, d, n = ada_w.shape
    cc = jnp.concatenate([c, c_ctx[None, :]], axis=0)
    cb = jnp.broadcast_to(cc[:, :, None], (MOD_ROWS, d, LANES))
    return pl.pallas_call(
        functools.partial(_mods_kernel, tn=tn),
        out_shape=jax.ShapeDtypeStruct((depth, SUBLANES, n), F32),
        grid=(depth, n // tn),
        in_specs=[
            pl.BlockSpec((MOD_ROWS, d, LANES), lambda l, j: (0, 0, 0)),
            pl.BlockSpec((1, d, tn), lambda l, j: (l, 0, j)),
            pl.BlockSpec((1, 1, tn), lambda l, j: (l, 0, j)),
        ],
        out_specs=pl.BlockSpec((1, SUBLANES, tn), lambda l, j: (l, 0, j)),
        scratch_shapes=[pltpu.VMEM((MOD_ROWS, d, LANES), F32)],
        compiler_params=_cparams("arbitrary", "arbitrary"),
        name="mods",
    )(cb, ada_w, ada_b.reshape(depth, 1, n))


def _head_norm_rope(t, gain, cs, sn):
    ms = jnp.mean(t * t, axis=-1, keepdims=True)
    tn = t * lax.rsqrt(ms + EPS) * gain
    return tn * cs + pltpu.roll(tn, HEAD_DIM // 2, axis=1) * sn


def _inproj0_kernel(x_ref, mod_ref, g_ref, w_ref, wqk_ref, cs_ref, sn_ref, qg_ref, kg_ref,
                    q_ref, k_ref, v_ref, za_ref, y_ref, zb_ref, h_ref):
    d = D_MODEL
    mod = _mod_row(mod_ref, pl.program_id(0))
    h_ref[...] = _norm_mod(x_ref[0], g_ref[...], mod[:, 0:d], mod[:, d:2 * d]).astype(BF16)
    h = h_ref[...]
    cs = cs_ref[...]
    sn = sn_ref[...]

    kk = _dot(h, wqk_ref[:, 0:KV_W])
    vv = _dot(h, w_ref[:, KV_W:2 * KV_W])
    ones = jnp.ones((V_AUG - HEAD_DIM, kk.shape[0]), BF16)
    for j in range(N_KV_HEADS):
        kj = kk[:, j * HEAD_DIM:(j + 1) * HEAD_DIM]
        k_ref[0, j] = _head_norm_rope(kj, kg_ref[...], cs, sn).astype(BF16)
        vj = vv[:, j * HEAD_DIM:(j + 1) * HEAD_DIM]
        v_ref[0, j, 0, 0:HEAD_DIM, :] = vj.T.astype(BF16)
        v_ref[0, j, 0, HEAD_DIM:V_AUG, :] = ones

    seg = 512
    heads_per_seg = seg // HEAD_DIM
    for s in range(ATTN_W // seg):
        qq = _dot(h, wqk_ref[:, KV_W + s * seg:KV_W + (s + 1) * seg])
        for j in range(heads_per_seg):
            qj = qq[:, j * HEAD_DIM:(j + 1) * HEAD_DIM]
            qj = _head_norm_rope(qj, qg_ref[...], cs, sn) * Q_SCALE
            q_ref[0, s * heads_per_seg + j] = qj.astype(BF16)

    base = 2 * KV_W + ATTN_W
    for s in range(ATTN_W // seg):
        z = _dot(h, w_ref[:, base + s * seg:base + (s + 1) * seg])
        za_ref[0, :, s * seg:(s + 1) * seg] = _silu(z).astype(BF16)

    base += ATTN_W
    for s in range(CONV_CH // seg):
        a = _dot(h, w_ref[:, base + s * seg:base + (s + 1) * seg])
        b = _dot(h, w_ref[:, base + CONV_CH + s * seg:base + CONV_CH + (s + 1) * seg])
        y_ref[0, :, s * seg:(s + 1) * seg] = (a * _sigmoid(b)).astype(BF16)

    base += 2 * CONV_CH
    for s in range(CONV_CH // seg):
        z = _dot(h, w_ref[:, base + s * seg:base + (s + 1) * seg])
        zb_ref[0, :, s * seg:(s + 1) * seg] = _silu(z).astype(BF16)


def _inproj0(x, mods, g, w, wqk, cs, sn, qg, kg, n_ctx, tm=256):
    bsz, n, d = x.shape
    row = lambda b, i: (b, i, 0)
    head = lambda b, i: (b, 0, i, 0)
    tpc = ATTN_TK // tm
    return pl.pallas_call(
        _inproj0_kernel,
        out_shape=(
            jax.ShapeDtypeStruct((bsz, N_Q_HEADS, n, HEAD_DIM), BF16),
            jax.ShapeDtypeStruct((bsz, N_KV_HEADS, n + n_ctx, HEAD_DIM), BF16),
            jax.ShapeDtypeStruct((bsz, N_KV_HEADS, (n + n_ctx) // ATTN_TK, V_AUG, ATTN_TK), BF16),
            jax.ShapeDtypeStruct((bsz, n, ATTN_W), BF16),
            jax.ShapeDtypeStruct((bsz, n, CONV_CH), BF16),
            jax.ShapeDtypeStruct((bsz, n, CONV_CH), BF16),
        ),
        grid=(bsz, n // tm),
        in_specs=[
            pl.BlockSpec((1, tm, d), row),
            _mod_spec(0),
            _resident((1, d)),
            _resident((d, EV_IN)),
            _resident((d, KV_W + ATTN_W)),
            pl.BlockSpec((tm, HEAD_DIM), lambda b, i: (i, 0)),
            pl.BlockSpec((tm, HEAD_DIM), lambda b, i: (i, 0)),
            _resident((1, HEAD_DIM)),
            _resident((1, HEAD_DIM)),
        ],
        out_specs=(
            pl.BlockSpec((1, N_Q_HEADS, tm, HEAD_DIM), head),
            pl.BlockSpec((1, N_KV_HEADS, tm, HEAD_DIM), head),
            pl.BlockSpec((1, N_KV_HEADS, 1, V_AUG, tm), lambda b, i: (b, 0, i // tpc, 0, i % tpc)),
            pl.BlockSpec((1, tm, ATTN_W), row),
            pl.BlockSpec((1, tm, CONV_CH), row),
            pl.BlockSpec((1, tm, CONV_CH), row),
        ),
        scratch_shapes=[pltpu.VMEM((tm, d), BF16)],
        compiler_params=_cparams("parallel", "parallel"),
        name="inproj0",
    )(x, mods, g, w, wqk, cs, sn, qg, kg)


def _ctxkv_kernel(x_ref, mod_ref, g_ref, wk_ref, wv_ref, kg_ref, k_in, v_in, k_ref, v_ref, *, mod_row):
    del k_in, v_in
    d = D_MODEL
    mod = _mod_row(mod_ref, mod_row)
    h = _norm_mod(x_ref[0], g_ref[...], mod[:, 0:d], mod[:, d:2 * d]).astype(BF16)
    kk = _dot(h, wk_ref[...])
    vv = _dot(h, wv_ref[...])
    ones = jnp.ones((V_AUG - HEAD_DIM, kk.shape[0]), BF16)
    for j in range(N_KV_HEADS):
        kj = kk[:, j * HEAD_DIM:(j + 1) * HEAD_DIM]
        ms = jnp.mean(kj * kj, axis=-1, keepdims=True)
        k_ref[0, j] = (kj * lax.rsqrt(ms + EPS) * kg_ref[...]).astype(BF16)
        vj = vv[:, j * HEAD_DIM:(j + 1) * HEAD_DIM]
        v_ref[0, j, 0, 0:HEAD_DIM, :] = vj.T.astype(BF16)
        v_ref[0, j, 0, HEAD_DIM:V_AUG, :] = ones


def _ctxkv(ctx, mods, g, w, wqk, kg, k_all, v_all):
    bsz, lc, d = ctx.shape
    n = k_all.shape[2] - lc
    tail = lambda b: (b, 0, n // lc, 0)
    return pl.pallas_call(
        functools.partial(_ctxkv_kernel, mod_row=bsz),
        out_shape=(
            jax.ShapeDtypeStruct(k_all.shape, k_all.dtype),
            jax.ShapeDtypeStruct(v_all.shape, v_all.dtype),
        ),
        grid=(bsz,),
        in_specs=[
            pl.BlockSpec((1, lc, d), lambda b: (b, 0, 0)),
            _mod_spec(0),
            pl.BlockSpec((1, d), lambda b: (0, 0)),
            pl.BlockSpec((d, KV_W), lambda b: (0, 0)),
            pl.BlockSpec((d, KV_W), lambda b: (0, 1)),
            pl.BlockSpec((1, HEAD_DIM), lambda b: (0, 0)),
            pl.BlockSpec(memory_space=pl.ANY),
            pl.BlockSpec(memory_space=pl.ANY),
        ],
        out_specs=(
            pl.BlockSpec((1, N_KV_HEADS, lc, HEAD_DIM), tail),
            pl.BlockSpec((1, N_KV_HEADS, 1, V_AUG, lc), lambda b: (b, 0, n // ATTN_TK, 0, (n % ATTN_TK) // lc)),
        ),
        input_output_aliases={6: 0, 7: 1},
        compiler_params=_cparams("parallel"),
        name="ctxkv",
    )(ctx, mods, g, wqk, w, kg, k_all, v_all)


def _attn_kernel(q_ref, qn_ref, k_ref, vt_ref, za_ref, o_ref, s_ref, acc_ref, m_ref, *, tk):
    i = pl.program_id(2)
    n_chunks = k_ref.shape[2] // tk
    assert n_chunks % 2 == 1
    q = q_ref[0, 0]

    def scores(qt, c):
        k0 = pl.multiple_of(c * tk, tk)
        kb = k_ref[0, 0, pl.ds(k0, tk), :]
        return lax.dot_general(kb, qt, (((1,), (1,)), ((), ())), preferred_element_type=F32)

    def softmax_pv(s, c):
        m_prev = m_ref[...]
        m_new = jnp.maximum(m_prev, jnp.max(s, axis=0, keepdims=True))
        alpha = jnp.exp2(m_prev - m_new)
        p = jnp.exp2(s - m_new[0:1]).astype(BF16)
        pv = _dot(vt_ref[0, 0, c], p)
        acc_ref[...] = alpha[0:1] * acc_ref[...] + pv
        m_ref[...] = m_new

    @pl.when(i == 0)
    def _():
        s_ref[0] = scores(q, 0)

    m_ref[...] = jnp.full(m_ref.shape, -jnp.inf, F32)
    acc_ref[...] = jnp.zeros(acc_ref.shape, F32)

    def sweep(cur):
        def chunk(c, slot, next_scores):
            s = s_ref[slot]
            s_ref[1 - slot] = next_scores()
            softmax_pv(s, c)

        def body(t, carry):
            for u in range(ATTN_UNROLL):
                c = ATTN_UNROLL * t + u
                chunk(c, (cur + u) % 2, functools.partial(scores, q, c + 1))
            return carry

        n_loop = (n_chunks - 1) // ATTN_UNROLL
        lax.fori_loop(0, n_loop, body, 0)
        for c in range(n_loop * ATTN_UNROLL, n_chunks - 1):
            chunk(c, (cur + c) % 2, functools.partial(scores, q, c + 1))
        chunk(n_chunks - 1, cur, functools.partial(scores, qn_ref[0, 0], 0))

    for par in range(2):
        pl.when(i % 2 == par)(functools.partial(sweep, par))

    acc = acc_ref[...]
    o = (acc[0:HEAD_DIM] / acc[HEAD_DIM:HEAD_DIM + 1]).T
    o_ref[0] = (o * za_ref[0].astype(F32)).astype(BF16)


def _attention(q, k, vt, za, tq=512):
    bsz, nh, n, _ = q.shape
    nk = k.shape[2]
    n_chunks, _, tk = vt.shape[2:]
    assert nk == n_chunks * tk and n % tq == 0
    nq = n // tq
    return pl.pallas_call(
        functools.partial(_attn_kernel, tk=tk),
        out_shape=jax.ShapeDtypeStruct((bsz, n, ATTN_W), BF16),
        grid=(bsz, nh, nq),
        in_specs=[
            pl.BlockSpec((1, 1, tq, HEAD_DIM), lambda b, h, i: (b, h, i, 0)),
            pl.BlockSpec((1, 1, tq, HEAD_DIM), lambda b, h, i: (b, h, jnp.minimum(i + 1, nq - 1), 0)),
            pl.BlockSpec((1, 1, nk, HEAD_DIM), lambda b, h, i: (b, h // Q_PER_KV, 0, 0)),
            pl.BlockSpec((1, 1, n_chunks, V_AUG, tk), lambda b, h, i: (b, h // Q_PER_KV, 0, 0, 0)),
            pl.BlockSpec((1, tq, HEAD_DIM), lambda b, h, i: (b, i, h)),
        ],
        out_specs=pl.BlockSpec((1, tq, HEAD_DIM), lambda b, h, i: (b, i, h)),
        scratch_shapes=[
            pltpu.VMEM((2, tk, tq), F32),
            pltpu.VMEM((V_AUG, tq), F32),
            pltpu.VMEM((SUBLANES, tq), F32),
        ],
        compiler_params=_cparams("parallel", "parallel", "arbitrary"),
        name="attn",
    )(q, q, k, vt, za)


CONV_RB = 128
CONV_NC = CONV_CH // LANES


def _conv_kernel(yp_ref, yc_ref, yn_ref, w_ref, b_ref, o_ref, win_ref, *, tm):
    i = pl.program_id(1)
    last = pl.num_programs(1) - 1
    prev = jnp.where(i > 0, yp_ref[0].astype(F32), 0.0)
    cur = yc_ref[0].astype(F32)
    nxt = jnp.where(i < last, yn_ref[0].astype(F32), 0.0)
    for c in range(CONV_NC):
        lanes = slice(c * LANES, (c + 1) * LANES)
        win_ref[c, 0:CONV_HALO] = prev[:, lanes]
        win_ref[c, CONV_HALO:CONV_HALO + tm] = cur[:, lanes]
        win_ref[c, CONV_HALO + tm:2 * CONV_HALO + tm] = nxt[:, lanes]

    off = CONV_HALO - CONV_WIDTH // 2

    def block(idx, carry):
        c = idx % CONV_NC
        base = pl.multiple_of((idx // CONV_NC) * CONV_RB, CONV_RB)
        out = None
        for shift in range(SUBLANES):
            part = None
            for j in range(CONV_WIDTH):
                if (off + j) % SUBLANES != shift:
                    continue
                rows = pl.ds(base + (off + j - shift), CONV_RB + SUBLANES)
                term = win_ref[c, rows, :] * w_ref[c, j:j + 1, :]
                part = term if part is None else part + term
            part = part[shift:shift + CONV_RB]
            out = part if out is None else out + part
        o_ref[0, c, pl.ds(base, CONV_RB), :] = out + b_ref[c]
        return carry

    lax.fori_loop(0, (tm // CONV_RB) * CONV_NC, block, 0)


def _conv(y, dw_w, dw_b, tm=512):
    bsz, n, ch = y.shape
    hb = tm // CONV_HALO
    nh = n // CONV_HALO
    w3 = dw_w.reshape(CONV_WIDTH, CONV_NC, LANES).transpose(1, 0, 2)
    b3 = dw_b.reshape(CONV_NC, 1, LANES)
    return pl.pallas_call(
        functools.partial(_conv_kernel, tm=tm),
        out_shape=jax.ShapeDtypeStruct((bsz, CONV_NC, n, LANES), F32),
        grid=(bsz, n // tm),
        in_specs=[
            pl.BlockSpec((1, CONV_HALO, ch), lambda b, i: (b, jnp.maximum(i * hb - 1, 0), 0)),
            pl.BlockSpec((1, tm, ch), lambda b, i: (b, i, 0)),
            pl.BlockSpec((1, CONV_HALO, ch), lambda b, i: (b, jnp.minimum((i + 1) * hb, nh - 1), 0)),
            pl.BlockSpec((CONV_NC, CONV_WIDTH, LANES), lambda b, i: (0, 0, 0)),
            pl.BlockSpec((CONV_NC, 1, LANES), lambda b, i: (0, 0, 0)),
        ],
        out_specs=pl.BlockSpec((1, CONV_NC, tm, LANES), lambda b, i: (b, 0, i, 0)),
        scratch_shapes=[pltpu.VMEM((CONV_NC, tm + 2 * CONV_HALO, LANES), F32)],
        compiler_params=_cparams("parallel", "parallel"),
        name="conv",
    )(y, y, y, w3, b3)


def _outproj_kernel(*refs, k_sizes, tn, final_norm):
    parts = refs[:len(k_sizes)]
    rest = refs[len(k_sizes):]
    if final_norm:
        w_ref, x_ref, mod_ref, fg_ref, o_ref = rest
    else:
        w_ref, x_ref, mod_ref, o_ref = rest
    d = D_MODEL
    gate = _mod_row(mod_ref, pl.program_id(0))[:, 2 * d:3 * d]
    ss = None
    for n in range(d // tn):
        cols = slice(n * tn, (n + 1) * tn)
        y = None
        k0 = 0
        for p_ref, ks in zip(parts, k_sizes):
            t = _dot(p_ref[0], w_ref[k0:k0 + ks, cols])
            y = t if y is None else y + t
            k0 += ks
        xn = x_ref[0, :, cols] + gate[:, cols] * y
        o_ref[0, :, cols] = xn
        if final_norm:
            sq = jnp.sum(xn * xn, axis=-1, keepdims=True)
            ss = sq if ss is None else ss + sq
    if final_norm:
        o_ref[0] = o_ref[0] * lax.rsqrt(ss * (1.0 / d) + EPS) * fg_ref[...]


def _outproj0_kernel(attn_ref, cv_ref, zb_ref, lg_ref, lb_ref, w_ref, x_ref, mod_ref, o_ref, cg_ref, *, tn):
    d = D_MODEL
    gate = _mod_row(mod_ref, pl.program_id(0))[:, 2 * d:3 * d]
    for n in range(d // tn):
        cols = slice(n * tn, (n + 1) * tn)
        o_ref[0, :, cols] = x_ref[0, :, cols] + gate[:, cols] * _dot(attn_ref[0], w_ref[0:ATTN_W, cols])

    cv = jnp.concatenate([cv_ref[0, c] for c in range(CONV_NC)], axis=1)
    mu = jnp.mean(cv, axis=-1, keepdims=True)
    cen = cv - mu
    var = jnp.mean(cen * cen, axis=-1, keepdims=True)
    yln = cen * lax.rsqrt(var + EPS) * lg_ref[...] + lb_ref[...]
    cg_ref[...] = (_silu(yln) * zb_ref[0].astype(F32)).astype(BF16)

    for n in range(d // tn):
        cols = slice(n * tn, (n + 1) * tn)
        o_ref[0, :, cols] += gate[:, cols] * _dot(cg_ref[...], w_ref[ATTN_W:ATTN_W + CONV_CH, cols])


def _outproj0(attn, cv, zb, ln_g, ln_b, w, x, mods, tm=512, tn=512):
    bsz, n, d = x.shape
    row = lambda b, i: (b, i, 0)
    return pl.pallas_call(
        functools.partial(_outproj0_kernel, tn=tn),
        out_shape=jax.ShapeDtypeStruct((bsz, n, d), F32),
        grid=(bsz, n // tm),
        in_specs=[
            pl.BlockSpec((1, tm, ATTN_W), row),
            pl.BlockSpec((1, CONV_NC, tm, LANES), lambda b, i: (b, 0, i, 0)),
            pl.BlockSpec((1, tm, CONV_CH), row),
            _resident((1, CONV_CH)),
            _resident((1, CONV_CH)),
            _resident(w.shape),
            pl.BlockSpec((1, tm, d), row),
            _mod_spec(0),
        ],
        out_specs=pl.BlockSpec((1, tm, d), row),
        scratch_shapes=[pltpu.VMEM((tm, CONV_CH), BF16)],
        compiler_params=_cparams("parallel", "parallel"),
        name="outproj0",
    )(attn, cv, zb, ln_g, ln_b, w, x, mods)


def _outproj(parts, w, x, mods, layer, final_g=None, tm=512, tn=512):
    bsz, n, d = x.shape
    k_sizes = tuple(p.shape[-1] for p in parts)
    row = lambda b, i: (b, i, 0)
    in_specs = [pl.BlockSpec((1, tm, ks), row) for ks in k_sizes]
    in_specs += [
        _resident(w.shape),
        pl.BlockSpec((1, tm, d), row),
        _mod_spec(layer),
    ]
    args = list(parts) + [w, x, mods]
    if final_g is not None:
        in_specs.append(_resident((1, d)))
        args.append(final_g)
    return pl.pallas_call(
        functools.partial(_outproj_kernel, k_sizes=k_sizes, tn=tn, final_norm=final_g is not None),
        out_shape=jax.ShapeDtypeStruct((bsz, n, d), F32),
        grid=(bsz, n // tm),
        in_specs=in_specs,
        out_specs=pl.BlockSpec((1, tm, d), row),
        compiler_params=_cparams("parallel", "parallel"),
        name="outproj_final" if final_g is not None else "outproj",
    )(*args)


def _sgu_kernel(x_ref, mod_ref, g_ref, w_ref, lng_ref, lnb_ref, ws_ref, bsb_ref, o_ref,
                h_ref, v_ref, vb_ref, uz_ref, *, tm):
    d = D_MODEL
    mod = _mod_row(mod_ref, pl.program_id(0))
    h_ref[...] = _norm_mod(x_ref[0], g_ref[...], mod[:, 0:d], mod[:, d:2 * d]).astype(BF16)
    h = h_ref[...]

    seg = 512
    for s in range(SGU_W // seg):
        cols = slice(s * seg, (s + 1) * seg)
        v_ref[:, cols] = _gelu_tanh(_dot(h, w_ref[:, SGU_W + s * seg:SGU_W + (s + 1) * seg]))
    for g in range(SGU_GROUPS):
        cols = slice(g * SGU_GW, (g + 1) * SGU_GW)
        u = _gelu_tanh(_dot(h, w_ref[:, g * SGU_GW:(g + 1) * SGU_GW]))
        z = _silu(_dot(h, w_ref[:, 2 * SGU_W + g * SGU_GW:2 * SGU_W + (g + 1) * SGU_GW]))
        uz_ref[:, cols] = u * z

    v = v_ref[...]
    mu = jnp.mean(v, axis=-1, keepdims=True)
    cen = v - mu
    var = jnp.mean(cen * cen, axis=-1, keepdims=True)
    vb_ref[...] = (cen * lax.rsqrt(var + EPS) * lng_ref[...] + lnb_ref[...]).astype(BF16)

    for g in range(SGU_GROUPS):
        cols = slice(g * SGU_GW, (g + 1) * SGU_GW)
        for c in range(tm // CHUNK):
            rws = slice(c * CHUNK, (c + 1) * CHUNK)
            mixed = _dot(ws_ref[g], vb_ref[rws, cols]) + bsb_ref[:, cols]
            o_ref[0, rws, cols] = (uz_ref[rws, cols] * mixed).astype(BF16)


def _sgu(x, mods, g, w, ln_g, ln_b, ws, bsb, tm=256):
    bsz, n, d = x.shape
    row = lambda b, i: (b, i, 0)
    return pl.pallas_call(
        functools.partial(_sgu_kernel, tm=tm),
        out_shape=jax.ShapeDtypeStruct((bsz, n, SGU_W), BF16),
        grid=(bsz, n // tm),
        in_specs=[
            pl.BlockSpec((1, tm, d), row),
            _mod_spec(1),
            _resident((1, d)),
            _resident((d, OD_IN)),
            _resident((1, SGU_W)),
            _resident((1, SGU_W)),
            _resident((SGU_GROUPS, CHUNK, CHUNK)),
            _resident((CHUNK, SGU_W)),
        ],
        out_specs=pl.BlockSpec((1, tm, SGU_W), row),
        scratch_shapes=[
            pltpu.VMEM((tm, d), BF16),
            pltpu.VMEM((tm, SGU_W), F32),
            pltpu.VMEM((tm, SGU_W), BF16),
            pltpu.VMEM((tm, SGU_W), F32),
        ],
        compiler_params=_cparams("parallel", "parallel"),
        name="sgu",
    )(x, mods, g, w, ln_g, ln_b, ws, bsb)


def _deinterleave_heads(wc, n_heads):
    lead = wc.shape[:-1]
    t = wc.reshape(lead + (n_heads, HEAD_DIM // 2, 2))
    return jnp.swapaxes(t, -1, -2).reshape(lead + (n_heads * HEAD_DIM,))


def _rope_tables(n):
    rows = n // GRID_W
    row = np.repeat(np.arange(rows, dtype=np.float64), GRID_W)
    col = np.tile(np.arange(GRID_W, dtype=np.float64), rows)
    inv = np.power(ROPE_THETA, np.arange(N_FREQ, dtype=np.float64) * (-2.0 / AXIS_DIM))
    ang = np.concatenate([row[:, None] * inv, col[:, None] * inv], axis=-1)
    cos, sin = np.cos(ang), np.sin(ang)
    cs = np.concatenate([cos, cos], axis=-1).astype(np.float32)
    sn = np.concatenate([-sin, sin], axis=-1).astype(np.float32)
    return jnp.asarray(cs), jnp.asarray(sn)


def kernel(x, c, ctx, c_ctx, ada_w, ada_b, norm_g, ev_w_in, ev_q_norm, ev_k_norm, ev_dw_w, ev_dw_b,
           ev_ln_g, ev_ln_b, ev_w_out, od_w_in, od_ln_g, od_ln_b, od_ws, od_bs, od_w_out, final_g):
    bsz, n, d = x.shape
    assert bsz + 1 == MOD_ROWS

    mods = _mods(c, c_ctx, ada_w, ada_b)

    w_in = ev_w_in[0]
    w0 = w_in.astype(BF16)
    wqk = jnp.concatenate([
        _deinterleave_heads(w_in[:, :KV_W], N_KV_HEADS),
        _deinterleave_heads(w_in[:, 2 * KV_W:2 * KV_W + ATTN_W], N_Q_HEADS),
    ], axis=1).astype(BF16)
    qg = _deinterleave_heads(ev_q_norm[0], 1)[None, :]
    kg = _deinterleave_heads(ev_k_norm[0], 1)[None, :]
    cs, sn = _rope_tables(n)
    g0 = norm_g[0][None, :]
    g1 = norm_g[1][None, :]

    q, k, v, za, y, zb = _inproj0(x, mods, g0, w0, wqk, cs, sn, qg, kg, ctx.shape[1])
    k, v = _ctxkv(ctx, mods, g0, w0, wqk, kg, k, v)
    attn = _attention(q, k, v, za)
    cv = _conv(y, ev_dw_w[0], ev_dw_b[0])
    x1 = _outproj0(attn, cv, zb, ev_ln_g[0][None, :], ev_ln_b[0][None, :], ev_w_out[0].astype(BF16), x, mods)

    bsb = jnp.repeat(od_bs[0].T, SGU_GW, axis=1)
    mixed = _sgu(x1, mods, g1, od_w_in[0].astype(BF16), od_ln_g[0][None, :], od_ln_b[0][None, :],
                 od_ws[0].astype(BF16), bsb)
    return _outproj([mixed], od_w_out[0].astype(BF16), x1, mods, 1, final_g=final_g[None, :])
```
